```python
import math
import jax, jax.numpy as jnp
from jax import lax
import numpy as np

D_MODEL = 2048
BATCH = 4
SEQ = 4096
DEPTH = 2

QBLOCK = 128
RMS_EPS = 1e-6
NEG_INF = -1e30
D_FF = 5632

MLA_HEADS = 4
MLA_Q_RANK = 512
MLA_KV_RANK = 256
MLA_NOPE = 128
MLA_ROPE = 64
MLA_V = 128
ROPE_THETA = 10000.0

DIFF_HEADS = 4
DIFF_QK = 64
DIFF_V = 128

NSA_HEADS = 4
NSA_D = 128
NSA_CMP_LEN = 32
NSA_CMP_STRIDE = 16
NSA_SEL_BLOCK = 64
NSA_N_SELECT = 16
NSA_WINDOW = 512
NSA_FORCE_BONUS = 1e4

SB_HEADS = 4
SB_D = 128

N_ALIBI_HEADS = DIFF_HEADS + NSA_HEADS
MIX_WIDTH = MLA_HEADS * MLA_V + DIFF_HEADS * DIFF_V + NSA_HEADS * NSA_D + SB_HEADS * SB_D

IN_SIZES = (
    MLA_Q_RANK, MLA_KV_RANK, MLA_ROPE,
    DIFF_HEADS * 2 * DIFF_QK, DIFF_HEADS * 2 * DIFF_QK, DIFF_HEADS * DIFF_V,
    NSA_HEADS * NSA_D, NSA_D, NSA_D, NSA_D, NSA_D, NSA_D, NSA_D, NSA_HEADS * 3,
    SB_HEADS * SB_D, SB_HEADS * SB_D, SB_HEADS * SB_D,
)
D_IN = sum(IN_SIZES)

kernel_name = 'hybrid_mla_diff_nsa_stickbreak_macaron'


def _in_offsets():
    offs, acc = [], 0
    for w in IN_SIZES[:-1]:
        acc += w
        offs.append(acc)
    return offs


def _rms(x, g):
    xf = x.astype(jnp.float32)
    y = xf * lax.rsqrt(jnp.mean(xf * xf, axis=-1, keepdims=True) + RMS_EPS)
    return (y * g.astype(jnp.float32)).astype(x.dtype)


def _swiglu(x, w_gate, w_up, w_down):
    return (jax.nn.silu(x @ w_gate) * (x @ w_up)) @ w_down


def _heads(x, h):
    b, s, c = x.shape
    return x.reshape(b, s, h, c // h).transpose(0, 2, 1, 3)


def _merge(x):
    b, h, s, d = x.shape
    return x.transpose(0, 2, 1, 3).reshape(b, s, h * d)


def _to_qblocks(x):
    b, h, s, d = x.shape
    return x.reshape(b, h, s // QBLOCK, QBLOCK, d).transpose(2, 0, 1, 3, 4)


def _from_qblocks(y):
    nb, b, h, qb, d = y.shape
    return y.transpose(1, 2, 0, 3, 4).reshape(b, h, nb * qb, d)


def _sweep(fn, *qs):
    nb = qs[0].shape[2] // QBLOCK
    out = lax.map(lambda a: fn(*a), (jnp.arange(nb), *[_to_qblocks(z) for z in qs]))
    return _from_qblocks(out)


def _masked_softmax(s, mask):
    s = jnp.where(mask, s.astype(jnp.float32), NEG_INF)
    m = jnp.max(s, axis=-1, keepdims=True)
    e = jnp.where(mask, jnp.exp(s - m), 0.0)
    return e / jnp.maximum(jnp.sum(e, axis=-1, keepdims=True), 1e-30)


def _rope(x, cos, sin):
    xf = x.astype(jnp.float32)
    x1, x2 = jnp.split(xf, 2, axis=-1)
    return jnp.concatenate([x1 * cos - x2 * sin, x2 * cos + x1 * sin], axis=-1).astype(x.dtype)


def _alibi_slopes(n):
    return 2.0 ** (-8.0 * jnp.arange(1, n + 1, dtype=jnp.float32) / n)


def _mla(c_q, c_kv, k_rope, cq_norm, ckv_norm, w_uq, w_ukv, qn_norm, qr_norm, kn_norm, kr_norm, o_norm, cos, sin):
    q = _heads(_rms(c_q, cq_norm) @ w_uq, MLA_HEADS)
    kv = _heads(_rms(c_kv, ckv_norm) @ w_ukv, MLA_HEADS)
    q_nope = _rms(q[..., :MLA_NOPE], qn_norm)
    q_rope = _rope(_rms(q[..., MLA_NOPE:], qr_norm), cos, sin)
    k_nope = _rms(kv[..., :MLA_NOPE], kn_norm)
    v = kv[..., MLA_NOPE:]
    k_r = _rope(_rms(k_rope, kr_norm), cos, sin)
    s = q.shape[2]
    kpos = jnp.arange(s)
    scale = (MLA_NOPE + MLA_ROPE) ** -0.5

    def block(i, qn, qr):
        t = i * QBLOCK + jnp.arange(QBLOCK)
        sc = (jnp.einsum('bhqd,bhkd->bhqk', qn, k_nope).astype(jnp.float32)
              + jnp.einsum('bhqr,bkr->bhqk', qr, k_r).astype(jnp.float32)) * scale
        p = _masked_softmax(sc, kpos[None, :] <= t[:, None])
        return jnp.einsum('bhqk,bhkd->bhqd', p.astype(v.dtype), v)

    o = _sweep(block, q_nope, q_rope)
    return _merge(_rms(o, o_norm))


def _diff(q, k, v, q_norm, k_norm, lq1, lk1, lq2, lk2, subln, slopes, lambda_init):
    b, s, _ = q.shape
    f32 = jnp.float32

    def split2(z):
        z = z.reshape(b, s, DIFF_HEADS, 2, DIFF_QK).transpose(3, 0, 2, 1, 4)
        return z[0], z[1]

    q1, q2 = split2(q)
    k1, k2 = split2(k)
    q1, q2 = _rms(q1, q_norm), _rms(q2, q_norm)
    k1, k2 = _rms(k1, k_norm), _rms(k2, k_norm)
    vh = _heads(v, DIFF_HEADS)
    lam = (jnp.exp(jnp.sum(lq1.astype(f32) * lk1.astype(f32)))
           - jnp.exp(jnp.sum(lq2.astype(f32) * lk2.astype(f32))) + lambda_init)
    kpos = jnp.arange(s)
    scale = DIFF_QK ** -0.5

    def block(i, q1b, q2b):
        t = i * QBLOCK + jnp.arange(QBLOCK)
        mask = kpos[None, :] <= t[:, None]
        bias = -slopes[:, None, None] * (t[:, None] - kpos[None, :]).astype(f32)
        s1 = jnp.einsum('bhqd,bhkd->bhqk', q1b, k1).astype(f32) * scale + bias
        s2 = jnp.einsum('bhqd,bhkd->bhqk', q2b, k2).astype(f32) * scale + bias
        p = _masked_softmax(s1, mask) - lam * _masked_softmax(s2, mask)
        return jnp.einsum('bhqk,bhkd->bhqd', p.astype(vh.dtype), vh)

    o = _sweep(block, q1, q2)
    return _merge(_rms(o, subln) * (1.0 - lambda_init))


def _nsa(q, kc_raw, vc_raw, ks, vs, kw, vw, gate_logits, q_norm, pe_k, w_ck, pe_v, w_cv,
         kc_norm, ks_norm, kw_norm, o_norm, slopes):
    b, s, _ = q.shape
    f32 = jnp.float32
    qh = _rms(_heads(q, NSA_HEADS), q_norm)
    gates = jax.nn.sigmoid(gate_logits.astype(f32)).reshape(b, s, NSA_HEADS, 3).transpose(0, 2, 1, 3)

    n_cmp = (s - NSA_CMP_LEN) // NSA_CMP_STRIDE + 1
    c_start = NSA_CMP_STRIDE * jnp.arange(n_cmp)
    tok = c_start[:, None] + jnp.arange(NSA_CMP_LEN)[None, :]
    c_end = c_start + NSA_CMP_LEN - 1

    def compress(z, pe, w):
        blocks = z[:, tok] + pe
        return blocks.reshape(b, n_cmp, NSA_CMP_LEN * NSA_D) @ w

    kc = _rms(compress(kc_raw, pe_k, w_ck), kc_norm)
    vc = compress(vc_raw, pe_v, w_cv)

    n_sb = s // NSA_SEL_BLOCK
    n_sel = min(NSA_N_SELECT, n_sb)
    ks_b = _rms(ks, ks_norm).reshape(b, n_sb, NSA_SEL_BLOCK, NSA_D)
    vs_b = vs.reshape(b, n_sb, NSA_SEL_BLOCK, NSA_D)
    sel_start = NSA_SEL_BLOCK * jnp.arange(n_sb)
    overlap = ((c_start[:, None] < sel_start[None, :] + NSA_SEL_BLOCK)
               & (c_start[:, None] + NSA_CMP_LEN > sel_start[None, :])).astype(f32)

    kw_p = jnp.pad(_rms(kw, kw_norm), ((0, 0), (NSA_WINDOW, 0), (0, 0)))
    vw_p = jnp.pad(vw, ((0, 0), (NSA_WINDOW, 0), (0, 0)))

    scale = NSA_D ** -0.5
    slope4 = slopes[None, :, None, None]
    jb = jnp.arange(n_sb)

    def block(i, qb, gb):
        t = i * QBLOCK + jnp.arange(QBLOCK)
        tf = t.astype(f32)
        sc = (jnp.einsum('bhqd,bnd->bhqn', qb, kc).astype(f32) * scale
              - slope4 * (tf[:, None] - c_end[None, :].astype(f32)))
        pc = _masked_softmax(sc, c_end[None, :] <= t[:, None])
        oc = jnp.einsum('bhqn,bnd->bhqd', pc.astype(vc.dtype), vc).astype(f32)
        imp = jnp.einsum('bhqn,nj->bqj', pc, overlap)
        cur = t // NSA_SEL_BLOCK
        valid = sel_start[None, :] <= t[:, None]
        forced = (jb[None, :] == 0) | (jb[None, :] == cur[:, None]) | (jb[None, :] == cur[:, None] - 1)
        score = jnp.where(valid, imp + jnp.where(forced, NSA_FORCE_BONUS, 0.0), NEG_INF)
        top_val, top_idx = lax.top_k(score, n_sel)
        picked = top_val > 0.5 * NEG_INF
        kg = jax.vmap(lambda kb_, ib_: kb_[ib_])(ks_b, top_idx)
        vg = jax.vmap(lambda vb_, ib_: vb_[ib_])(vs_b, top_idx)
        pos = top_idx[..., None] * NSA_SEL_BLOCK + jnp.arange(NSA_SEL_BLOCK)
        ms = picked[..., None] & (pos <= t[None, :, None, None])
        ss = (jnp.einsum('bhqd,bqnkd->bhqnk', qb, kg).astype(f32) * scale
              - slope4[..., None] * (tf[None, :, None, None] - pos.astype(f32))[:, None])
        m_tok = n_sel * NSA_SEL_BLOCK
        ps = _masked_softmax(ss.reshape(b, NSA_HEADS, QBLOCK, m_tok), ms.reshape(b, 1, QBLOCK, m_tok))
        osel = jnp.einsum('bhqm,bqmd->bhqd', ps.astype(vg.dtype), vg.reshape(b, QBLOCK, m_tok, NSA_D)).astype(f32)
        kwin = lax.dynamic_slice_in_dim(kw_p, i * QBLOCK, QBLOCK + NSA_WINDOW, axis=1)
        vwin = lax.dynamic_slice_in_dim(vw_p, i * QBLOCK, QBLOCK + NSA_WINDOW, axis=1)
        kp = i * QBLOCK - NSA_WINDOW + jnp.arange(QBLOCK + NSA_WINDOW)
        mw = (kp[None, :] <= t[:, None]) & (t[:, None] - kp[None, :] < NSA_WINDOW) & (kp[None, :] >= 0)
        sw = (jnp.einsum('bhqd,bkd->bhqk', qb, kwin).astype(f32) * scale
              - slope4 * (tf[:, None] - kp[None, :].astype(f32)))
        pw = _masked_softmax(sw, mw)
        ow = jnp.einsum('bhqk,bkd->bhqd', pw.astype(vwin.dtype), vwin).astype(f32)
        g = gb.astype(f32)
        o = g[..., 0:1] * oc + g[..., 1:2] * osel + g[..., 2:3] * ow
        return o.astype(qb.dtype)

    o = _sweep(block, qh, gates)
    return _merge(_rms(o, o_norm))


def _stick_breaking(q, k, v, o_norm):
    qh, kh, vh = _heads(q, SB_HEADS), _heads(k, SB_HEADS), _heads(v, SB_HEADS)
    s = qh.shape[2]
    kpos = jnp.arange(s)
    scale = SB_D ** -0.5

    def block(i, qb):
        t = i * QBLOCK + jnp.arange(QBLOCK)
        z = jnp.einsum('bhqd,bhkd->bhqk', qb, kh).astype(jnp.float32) * scale
        mask = kpos[None, :] < t[:, None]
        log_not = jnp.where(mask, jax.nn.log_sigmoid(-z), 0.0)
        after = lax.cumsum(log_not, axis=3, reverse=True) - log_not
        a = jnp.where(mask, jnp.exp(jax.nn.log_sigmoid(z) + after), 0.0)
        return jnp.einsum('bhqk,bhkd->bhqd', a.astype(vh.dtype), vh)

    o = _sweep(block, qh)
    return _merge(_rms(o, o_norm))


def setup_inputs(seed: int = 0) -> dict:
    key = jax.random.key(seed)
    keys = iter(jax.random.split(key, 64))
    L, D, F = DEPTH, D_MODEL, D_FF
    f32 = jnp.float32

    def dense(shape, fan_in):
        return jax.random.normal(next(keys), shape, f32) * fan_in ** -0.5

    def gain(shape):
        return 1.0 + 0.05 * jax.random.normal(next(keys), shape, f32)

    def small(shape, sc):
        return sc * jax.random.normal(next(keys), shape, f32)

    return {
        'x': jax.random.normal(next(keys), (BATCH, SEQ, D), f32),
        'ffn1_norm': gain((L, D)),
        'ffn1_w_gate': dense((L, D, F), D),
        'ffn1_w_up': dense((L, D, F), D),
        'ffn1_w_down': dense((L, F, D), F),
        'mix_norm': gain((L, D)),
        'w_in': dense((L, D, D_IN), D),
        'mla_cq_norm': gain((L, MLA_Q_RANK)),
        'mla_ckv_norm': gain((L, MLA_KV_RANK)),
        'mla_w_uq': dense((L, MLA_Q_RANK, MLA_HEADS * (MLA_NOPE + MLA_ROPE)), MLA_Q_RANK),
        'mla_w_ukv': dense((L, MLA_KV_RANK, MLA_HEADS * (MLA_NOPE + MLA_V)), MLA_KV_RANK),
        'mla_qn_norm': gain((L, MLA_NOPE)),
        'mla_qr_norm': gain((L, MLA_ROPE)),
        'mla_kn_norm': gain((L, MLA_NOPE)),
        'mla_kr_norm': gain((L, MLA_ROPE)),
        'mla_o_norm': gain((L, MLA_V)),
        'diff_q_norm': gain((L, DIFF_QK)),
        'diff_k_norm': gain((L, DIFF_QK)),
        'diff_lq1': small((L, DIFF_QK), 0.1),
        'diff_lk1': small((L, DIFF_QK), 0.1),
        'diff_lq2': small((L, DIFF_QK), 0.1),
        'diff_lk2': small((L, DIFF_QK), 0.1),
        'diff_subln': gain((L, DIFF_V)),
        'nsa_q_norm': gain((L, NSA_D)),
        'nsa_pe_k': small((L, NSA_CMP_LEN, NSA_D), 0.1),
        'nsa_w_ck': dense((L, NSA_CMP_LEN * NSA_D, NSA_D), NSA_CMP_LEN * NSA_D),
        'nsa_pe_v': small((L, NSA_CMP_LEN, NSA_D), 0.1),
        'nsa_w_cv': dense((L, NSA_CMP_LEN * NSA_D, NSA_D), NSA_CMP_LEN * NSA_D),
        'nsa_kc_norm': gain((L, NSA_D)),
        'nsa_ks_norm': gain((L, NSA_D)),
        'nsa_kw_norm': gain((L, NSA_D)),
        'nsa_o_norm': gain((L, NSA_D)),
        'sb_o_norm': gain((L, SB_D)),
        'w_out': dense((L, MIX_WIDTH, D), MIX_WIDTH),
        'ffn2_norm': gain((L, D)),
        'ffn2_w_gate': dense((L, D, F), D),
        'ffn2_w_up': dense((L, D, F), D),
        'ffn2_w_down': dense((L, F, D), F),
    }


def reference(x, ffn1_norm, ffn1_w_gate, ffn1_w_up, ffn1_w_down, mix_norm, w_in,
              mla_cq_norm, mla_ckv_norm, mla_w_uq, mla_w_ukv, mla_qn_norm, mla_qr_norm,
              mla_kn_norm, mla_kr_norm, mla_o_norm,
              diff_q_norm, diff_k_norm, diff_lq1, diff_lk1, diff_lq2, diff_lk2, diff_subln,
              nsa_q_norm, nsa_pe_k, nsa_w_ck, nsa_pe_v, nsa_w_cv, nsa_kc_norm, nsa_ks_norm,
              nsa_kw_norm, nsa_o_norm, sb_o_norm, w_out,
              ffn2_norm, ffn2_w_gate, ffn2_w_up, ffn2_w_down):
    s = x.shape[1]
    pos = jnp.arange(s, dtype=jnp.float32)
    inv_freq = ROPE_THETA ** (-jnp.arange(0, MLA_ROPE, 2, dtype=jnp.float32) / MLA_ROPE)
    ang = pos[:, None] * inv_freq[None, :]
    cos, sin = jnp.cos(ang), jnp.sin(ang)
    slopes = _alibi_slopes(N_ALIBI_HEADS)
    diff_slopes, nsa_slopes = slopes[0::2], slopes[1::2]
    offsets = _in_offsets()

    h = x
    for l in range(DEPTH):
        h = h + 0.5 * _swiglu(_rms(h, ffn1_norm[l]), ffn1_w_gate[l], ffn1_w_up[l], ffn1_w_down[l])
        u = _rms(h, mix_norm[l]) @ w_in[l]
        (a_cq, a_ckv, a_kr, b_q, b_k, b_v, c_q, c_kc, c_vc, c_ks, c_vs, c_kw, c_vw, c_g,
         d_q, d_k, d_v) = jnp.split(u, offsets, axis=-1)
        lambda_init = 0.8 - 0.6 * math.exp(-0.3 * l)
        o_a = _mla(a_cq, a_ckv, a_kr, mla_cq_norm[l], mla_ckv_norm[l], mla_w_uq[l], mla_w_ukv[l],
                   mla_qn_norm[l], mla_qr_norm[l], mla_kn_norm[l], mla_kr_norm[l], mla_o_norm[l], cos, sin)
        o_b = _diff(b_q, b_k, b_v, diff_q_norm[l], diff_k_norm[l], diff_lq1[l], diff_lk1[l],
                    diff_lq2[l], diff_lk2[l], diff_subln[l], diff_slopes, lambda_init)
        o_c = _nsa(c_q, c_kc, c_vc, c_ks, c_vs, c_kw, c_vw, c_g, nsa_q_norm[l], nsa_pe_k[l], nsa_w_ck[l],
                   nsa_pe_v[l], nsa_w_cv[l], nsa_kc_norm[l], nsa_ks_norm[l], nsa_kw_norm[l],
                   nsa_o_norm[l], nsa_slopes)
        o_d = _stick_breaking(d_q, d_k, d_v, sb_o_norm[l])
        h = h + jnp.concatenate([o_a, o_b, o_c, o_d], axis=-1) @ w_out[l]
        h = h + 0.5 * _swiglu(_rms(h, ffn2_norm[l]), ffn2_w_gate[l], ffn2_w_up[l], ffn2_w_down[l])
    return h
```

```python
import functools
import math

import numpy as np
import jax
import jax.numpy as jnp
from jax import lax
from jax.experimental import pallas as pl
from jax.experimental.pallas import tpu as pltpu

F32 = jnp.float32
BF16 = jnp.bfloat16

RMS_EPS = 1e-6
NEG_INF = -1e30
LANES = 128
VMEM_LIMIT = 56 * 1024 * 1024

MLA_HEADS, MLA_Q_RANK, MLA_KV_RANK, MLA_NOPE, MLA_ROPE, MLA_V = 4, 512, 256, 128, 64, 128
ROPE_THETA = 10000.0
DIFF_HEADS, DIFF_QK, DIFF_V = 4, 64, 128
NSA_HEADS, NSA_D = 4, 128
NSA_CMP_LEN, NSA_CMP_STRIDE, NSA_SEL_BLOCK, NSA_N_SELECT, NSA_WINDOW = 32, 16, 64, 16, 512
NSA_FORCE_BONUS = 1e4
SB_HEADS, SB_D = 4, 128
N_ALIBI_HEADS = DIFF_HEADS + NSA_HEADS
ALIBI_SLOPES = [2.0 ** (-8.0 * k / N_ALIBI_HEADS) for k in range(1, N_ALIBI_HEADS + 1)]
DIFF_SLOPES, NSA_SLOPES = ALIBI_SLOPES[0::2], ALIBI_SLOPES[1::2]

W_MLA = MLA_Q_RANK + MLA_KV_RANK + MLA_ROPE
W_DIFF = DIFF_HEADS * (2 * DIFF_QK + 2 * DIFF_QK + DIFF_V)
W_NSA = NSA_HEADS * NSA_D + 6 * NSA_D + NSA_HEADS * 3
W_SB = 3 * SB_HEADS * SB_D


def _dot(a, b):
    return jnp.dot(a, b, preferred_element_type=F32)


def _dot_nt(a, b):
    return lax.dot_general(a, b, (((1,), (1,)), ((), ())), preferred_element_type=F32)


def _rms_rows(x, g):
    ms = jnp.mean(x * x, axis=-1, keepdims=True)
    return x * lax.rsqrt(ms + RMS_EPS) * g


def _params(*sem):
    return pltpu.CompilerParams(dimension_semantics=sem, vmem_limit_bytes=VMEM_LIMIT)


def _ffn_kernel(x_ref, g_ref, wg_ref, wu_ref, wd_ref, o_ref, xn_ref, acc_ref, *, nf):
    f = pl.program_id(1)

    @pl.when(f == 0)
    def _():
        xn_ref[...] = _rms_rows(x_ref[...], g_ref[...]).astype(BF16)
        acc_ref[...] = jnp.zeros_like(acc_ref)

    xn = xn_ref[...]
    gate = _dot(xn, wg_ref[...])
    up = _dot(xn, wu_ref[...])
    mid = (gate * jax.nn.sigmoid(gate) * up).astype(BF16)
    acc_ref[...] += _dot(mid, wd_ref[...])

    @pl.when(f == nf - 1)
    def _():
        o_ref[...] = x_ref[...] + 0.5 * acc_ref[...]


def _ffn(h, g, wg, wu, wd, tm=512, tf=512):
    m, d = h.shape
    f = wg.shape[1]
    return pl.pallas_call(
        functools.partial(_ffn_kernel, nf=f // tf),
        grid=(m // tm, f // tf),
        in_specs=[
            pl.BlockSpec((tm, d), lambda i, j: (i, 0)),
            pl.BlockSpec((1, d), lambda i, j: (0, 0)),
            pl.BlockSpec((d, tf), lambda i, j: (0, j)),
            pl.BlockSpec((d, tf), lambda i, j: (0, j)),
            pl.BlockSpec((tf, d), lambda i, j: (j, 0)),
        ],
        out_specs=pl.BlockSpec((tm, d), lambda i, j: (i, 0)),
        out_shape=jax.ShapeDtypeStruct((m, d), F32),
        scratch_shapes=[pltpu.VMEM((tm, d), BF16), pltpu.VMEM((tm, d), F32)],
        compiler_params=_params("parallel", "arbitrary"),
        name="ffn",
    )(h, g.reshape(1, d), wg, wu, wd)


def _out_proj_kernel(a0_ref, a1_ref, a2_ref, a3_ref, w_ref, r_ref, o_ref):
    acc = r_ref[...]
    for n, a_ref in enumerate((a0_ref, a1_ref, a2_ref, a3_ref)):
        kw = a_ref.shape[1]
        acc = acc + _dot(a_ref[...], w_ref[n * kw:(n + 1) * kw, :])
    o_ref[...] = acc


def _out_proj(parts, w, res, tm=512, tn=1024):
    m, d = res.shape
    kw = parts[0].shape[1]
    a_spec = pl.BlockSpec((tm, kw), lambda i, j: (i, 0))
    return pl.pallas_call(
        _out_proj_kernel,
        grid=(m // tm, d // tn),
        in_specs=[a_spec, a_spec, a_spec, a_spec,
                  pl.BlockSpec((w.shape[0], tn), lambda i, j: (0, j)),
                  pl.BlockSpec((tm, tn), lambda i, j: (i, j))],
        out_specs=pl.BlockSpec((tm, tn), lambda i, j: (i, j)),
        out_shape=jax.ShapeDtypeStruct((m, d), F32),
        compiler_params=_params("parallel", "arbitrary"),
        name="out_proj",
    )(*parts, w, res)


def _sb_prep_kernel(x_ref, g_ref, w_ref, o_ref):
    xn = _rms_rows(x_ref[...], g_ref[...]).astype(BF16)
    o_ref[...] = _dot(xn, w_ref[...]).astype(o_ref.dtype)


def _sb_prep(h, g, w, tm=512):
    m, d = h.shape
    n = w.shape[1]
    return pl.pallas_call(
        _sb_prep_kernel,
        grid=(m // tm,),
        in_specs=[pl.BlockSpec((tm, d), lambda i: (i, 0)),
                  pl.BlockSpec((1, d), lambda i: (0, 0)),
                  pl.BlockSpec((d, n), lambda i: (0, 0))],
        out_specs=pl.BlockSpec((tm, n), lambda i: (i, 0)),
        out_shape=jax.ShapeDtypeStruct((m, n), BF16),
        compiler_params=_params("parallel"),
        name="sb_prep",
    )(h, g.reshape(1, d), w)


def _rope_rows(x, x_sw, g, g_sw, cos2, sin2):
    ms = jnp.sum(x * x, axis=-1, keepdims=True) * (1.0 / MLA_ROPE)
    r = lax.rsqrt(ms + RMS_EPS)
    return (x * r * g) * cos2 + (x_sw * r * g_sw) * sin2


def _mla_prep_kernel(x_ref, g_ref, w_ref, cqn_ref, ckvn_ref, wuq_ref, wukv_ref, qn_ref, qr_ref,
                     qrs_ref, kn_ref, kr_ref, krs_ref, cos_ref, sin_ref, q_out, k_out, v_out):
    xn = _rms_rows(x_ref[...], g_ref[...]).astype(BF16)
    u = _dot(xn, w_ref[...])
    c_q = u[:, :MLA_Q_RANK]
    c_kv = u[:, MLA_Q_RANK:MLA_Q_RANK + MLA_KV_RANK]
    kr = u[:, MLA_Q_RANK + MLA_KV_RANK:MLA_Q_RANK + MLA_KV_RANK + LANES]
    kr_sw = u[:, MLA_Q_RANK + MLA_KV_RANK + LANES:]
    cos2, sin2 = cos_ref[...], sin_ref[...]
    qall = _dot(_rms_rows(c_q, cqn_ref[...]).astype(BF16), wuq_ref[...])
    kvall = _dot(_rms_rows(c_kv, ckvn_ref[...]).astype(BF16), wukv_ref[...])
    k_rot = _rope_rows(kr, kr_sw, kr_ref[...], krs_ref[...], cos2, sin2).astype(BF16)
    scale = (MLA_NOPE + MLA_ROPE) ** -0.5
    for h in range(MLA_HEADS):
        qh = qall[:, h * 3 * LANES:(h + 1) * 3 * LANES]
        q_nope = _rms_rows(qh[:, :LANES], qn_ref[...]) * scale
        q_rot = _rope_rows(qh[:, LANES:2 * LANES], qh[:, 2 * LANES:], qr_ref[...], qrs_ref[...],
                           cos2, sin2) * scale
        q_out[:, h * 2 * LANES:h * 2 * LANES + LANES] = q_nope.astype(BF16)
        q_out[:, h * 2 * LANES + LANES:(h + 1) * 2 * LANES] = q_rot.astype(BF16)
        kvh = kvall[:, h * 2 * LANES:(h + 1) * 2 * LANES]
        k_out[:, h * 2 * LANES:h * 2 * LANES + LANES] = _rms_rows(kvh[:, :LANES], kn_ref[...]).astype(BF16)
        k_out[:, h * 2 * LANES + LANES:(h + 1) * 2 * LANES] = k_rot
        v_out[:, h * LANES:(h + 1) * LANES] = kvh[:, LANES:].astype(BF16)


def _mla_prep(h3, g, w, cqn, ckvn, wuq, wukv, qn, qr, qrs, kn, kr, krs, cos2, sin2, tm=512):
    b, s, d = h3.shape
    full = lambda a: pl.BlockSpec(a.shape, lambda bi, i: (0,) * a.ndim)
    smalls = [g, w, cqn, ckvn, wuq, wukv, qn, qr, qrs, kn, kr, krs]
    tab = pl.BlockSpec((tm, LANES), lambda bi, i: (i, 0))
    out = lambda c: pl.BlockSpec((None, tm, c), lambda bi, i: (bi, i, 0))
    return pl.pallas_call(
        _mla_prep_kernel,
        grid=(b, s // tm),
        in_specs=[pl.BlockSpec((None, tm, d), lambda bi, i: (bi, i, 0))] + [full(a) for a in smalls] + [tab, tab],
        out_specs=[out(MLA_HEADS * 2 * LANES), out(MLA_HEADS * 2 * LANES), out(MLA_HEADS * MLA_V)],
        out_shape=[jax.ShapeDtypeStruct((b, s, MLA_HEADS * 2 * LANES), BF16),
                   jax.ShapeDtypeStruct((b, s, MLA_HEADS * 2 * LANES), BF16),
                   jax.ShapeDtypeStruct((b, s, MLA_HEADS * MLA_V), BF16)],
        compiler_params=_params("parallel", "parallel"),
        name="mla_prep",
    )(h3, *smalls, cos2, sin2)


def _diff_prep_kernel(x_ref, g_ref, w_ref, qn_ref, kn_ref, q1_out, q2_out, k_out, v_out):
    xn = _rms_rows(x_ref[...], g_ref[...]).astype(BF16)
    u = _dot(xn, w_ref[...])
    width = DIFF_HEADS * LANES
    lane = lax.broadcasted_iota(jnp.int32, (1, LANES), 1)
    lo = lane < DIFF_QK
    scale = DIFF_QK ** -0.5

    def half_rms(x, g2):
        sq = x * x
        s_lo = jnp.sum(jnp.where(lo, sq, 0.0), axis=-1, keepdims=True)
        s_hi = jnp.sum(jnp.where(lo, 0.0, sq), axis=-1, keepdims=True)
        r = jnp.where(lo, lax.rsqrt(s_lo * (1.0 / DIFF_QK) + RMS_EPS),
                      lax.rsqrt(s_hi * (1.0 / DIFF_QK) + RMS_EPS))
        return x * r * g2

    for h in range(DIFF_HEADS):
        sl = slice(h * LANES, (h + 1) * LANES)
        qn = half_rms(u[:, sl], qn_ref[...]) * scale
        q1_out[:, sl] = jnp.where(lo, qn, 0.0).astype(BF16)
        q2_out[:, sl] = jnp.where(lo, 0.0, qn).astype(BF16)
        k_out[:, sl] = half_rms(u[:, width + h * LANES:width + (h + 1) * LANES], kn_ref[...]).astype(BF16)
    v_out[...] = u[:, 2 * width:].astype(BF16)


def _diff_prep(h, g, w, qn2, kn2, tm=512):
    m, d = h.shape
    width = DIFF_HEADS * LANES
    full = lambda a: pl.BlockSpec(a.shape, lambda i: (0,) * a.ndim)
    out = pl.BlockSpec((tm, width), lambda i: (i, 0))
    return pl.pallas_call(
        _diff_prep_kernel,
        grid=(m // tm,),
        in_specs=[pl.BlockSpec((tm, d), lambda i: (i, 0)), full(g), full(w), full(qn2), full(kn2)],
        out_specs=[out] * 4,
        out_shape=[jax.ShapeDtypeStruct((m, width), BF16)] * 4,
        compiler_params=_params("parallel"),
        name="diff_prep",
    )(h, g, w, qn2, kn2)


def _nsa_prep_kernel(x_ref, g_ref, w_ref, qn_ref, ksn_ref, kwn_ref, q_out, kv_out, raw_out, gate_out):
    xn = _rms_rows(x_ref[...], g_ref[...]).astype(BF16)
    u = _dot(xn, w_ref[...])
    scale = NSA_D ** -0.5
    for h in range(NSA_HEADS):
        sl = slice(h * LANES, (h + 1) * LANES)
        q_out[:, sl] = (_rms_rows(u[:, sl], qn_ref[...]) * scale).astype(BF16)
    base = NSA_HEADS * LANES
    piece = lambda n: u[:, base + n * LANES:base + (n + 1) * LANES]
    raw_out[:, :LANES] = piece(0)
    raw_out[:, LANES:] = piece(1)
    kv_out[:, 0 * LANES:1 * LANES] = _rms_rows(piece(2), ksn_ref[...]).astype(BF16)
    kv_out[:, 1 * LANES:2 * LANES] = piece(3).astype(BF16)
    kv_out[:, 2 * LANES:3 * LANES] = _rms_rows(piece(4), kwn_ref[...]).astype(BF16)
    kv_out[:, 3 * LANES:4 * LANES] = piece(5).astype(BF16)
    gate_out[...] = jax.nn.sigmoid(piece(6))


def _nsa_prep(h, g, w, qn, ksn, kwn, tm=512):
    m, d = h.shape
    full = lambda a: pl.BlockSpec(a.shape, lambda i: (0,) * a.ndim)
    out = lambda c: pl.BlockSpec((tm, c), lambda i: (i, 0))
    return pl.pallas_call(
        _nsa_prep_kernel,
        grid=(m // tm,),
        in_specs=[pl.BlockSpec((tm, d), lambda i: (i, 0)), full(g), full(w), full(qn), full(ksn), full(kwn)],
        out_specs=[out(NSA_HEADS * LANES), out(4 * LANES), out(2 * LANES), out(LANES)],
        out_shape=[jax.ShapeDtypeStruct((m, NSA_HEADS * LANES), BF16),
                   jax.ShapeDtypeStruct((m, 4 * LANES), BF16),
                   jax.ShapeDtypeStruct((m, 2 * LANES), F32),
                   jax.ShapeDtypeStruct((m, LANES), F32)],
        compiler_params=_params("parallel"),
        name="nsa_prep",
    )(h, g, w, qn, ksn, kwn)


def _nsa_compress_kernel(zk_ref, zv_ref, pek_ref, pev_ref, wk_ref, wv_ref, kcn_ref, kc_out, vc_out):
    half = NSA_CMP_STRIDE * NSA_D
    nchunk = zk_ref.shape[0]

    def compress(z, pe_ref, w_ref):
        first = _dot((z + pe_ref[:, :half]).astype(BF16), w_ref[:half, :])
        second = _dot((z + pe_ref[:, half:]).astype(BF16), w_ref[half:, :])
        return first + pltpu.roll(second, shift=nchunk - 1, axis=0)

    kc_out[...] = _rms_rows(compress(zk_ref[...], pek_ref, wk_ref), kcn_ref[...]).astype(BF16)
    vc_out[...] = compress(zv_ref[...], pev_ref, wv_ref).astype(BF16)


def _nsa_compress(zk, zv, pek, pev, wk, wv, kcn):
    b, nchunk, width = zk.shape
    full = lambda a: pl.BlockSpec(a.shape, lambda bi: (0,) * a.ndim)
    z_spec = pl.BlockSpec((None, nchunk, width), lambda bi: (bi, 0, 0))
    o_spec = pl.BlockSpec((None, nchunk, NSA_D), lambda bi: (bi, 0, 0))
    return pl.pallas_call(
        _nsa_compress_kernel,
        grid=(b,),
        in_specs=[z_spec, z_spec, full(pek), full(pev), full(wk), full(wv), full(kcn)],
        out_specs=[o_spec, o_spec],
        out_shape=[jax.ShapeDtypeStruct((b, nchunk, NSA_D), BF16)] * 2,
        compiler_params=_params("parallel"),
        name="nsa_compress",
    )(zk, zv, pek, pev, wk, wv, kcn)


def _softmax_step(s, v, m, l, acc):
    m_new = jnp.maximum(m, jnp.max(s, axis=-1, keepdims=True))
    alpha = jnp.exp(m - m_new)
    p = jnp.exp(s - m_new)
    l = alpha * l + jnp.sum(p, axis=-1, keepdims=True)
    acc = alpha * acc + _dot(p.astype(BF16), v)
    return m_new, l, acc


def _softmax_init(tq, dv):
    return (jnp.full((tq, 1), NEG_INF, F32), jnp.zeros((tq, 1), F32), jnp.zeros((tq, dv), F32))


def _softmax_out(l, acc):
    return acc * (1.0 / jnp.maximum(l, 1e-30))


def _mla_attn_kernel(q_ref, k_ref, v_ref, on_ref, o_ref, *, tq, tk):
    i = pl.program_id(2)
    q = q_ref[...]
    nsub = tq // tk

    def tile(j):
        st = pl.multiple_of(j * tk, tk)
        return k_ref[pl.ds(st, tk), :], v_ref[pl.ds(st, tk), :]

    def body(j, c):
        k, v = tile(j)
        return _softmax_step(_dot_nt(q, k), v, *c)

    c = lax.fori_loop(0, i * nsub, body, _softmax_init(tq, MLA_V))
    row = lax.broadcasted_iota(jnp.int32, (tq, tk), 0)
    col = lax.broadcasted_iota(jnp.int32, (tq, tk), 1)
    for d in range(nsub):
        k, v = tile(i * nsub + d)
        s = jnp.where(col + d * tk <= row, _dot_nt(q, k), NEG_INF)
        c = _softmax_step(s, v, *c)
    _, l, acc = c
    o_ref[...] = _rms_rows(_softmax_out(l, acc), on_ref[...]).astype(o_ref.dtype)


def _mla_attn(q, k, v, on, tq=256, tk=256):
    b, s, _ = q.shape
    return pl.pallas_call(
        functools.partial(_mla_attn_kernel, tq=tq, tk=tk),
        grid=(b, MLA_HEADS, s // tq),
        in_specs=[pl.BlockSpec((None, tq, 2 * LANES), lambda bi, h, i: (bi, i, h)),
                  pl.BlockSpec((None, s, 2 * LANES), lambda bi, h, i: (bi, 0, h)),
                  pl.BlockSpec((None, s, MLA_V), lambda bi, h, i: (bi, 0, h)),
                  pl.BlockSpec((1, MLA_V), lambda bi, h, i: (0, 0))],
        out_specs=pl.BlockSpec((None, tq, MLA_V), lambda bi, h, i: (bi, i, h)),
        out_shape=jax.ShapeDtypeStruct((b, s, MLA_HEADS * MLA_V), BF16),
        compiler_params=_params("parallel", "parallel", "arbitrary"),
        name="mla_attn",
    )(q, k, v, on)


def _diff_attn_kernel(q1_ref, q2_ref, k_ref, v_ref, slope_ref, lq1_ref, lk1_ref, lq2_ref, lk2_ref,
                      sub_ref, o_ref, *, tq, tk, lambda_init):
    i = pl.program_id(2)
    q1, q2 = q1_ref[...], q2_ref[...]
    nsub = tq // tk
    slope = slope_ref[0]
    slope = slope[:, :1]
    t0 = i * tq
    colrel = lax.broadcasted_iota(jnp.int32, (1, tk), 1)

    def tile(j):
        st = pl.multiple_of(j * tk, tk)
        bias = slope * (colrel + (j * tk - t0)).astype(F32)
        return k_ref[pl.ds(st, tk), :], v_ref[pl.ds(st, tk), :], bias

    def body(j, c):
        k, v, bias = tile(j)
        c1 = _softmax_step(_dot_nt(q1, k) + bias, v, *c[:3])
        c2 = _softmax_step(_dot_nt(q2, k) + bias, v, *c[3:])
        return c1 + c2

    c = lax.fori_loop(0, i * nsub, body, _softmax_init(tq, DIFF_V) + _softmax_init(tq, DIFF_V))
    row = lax.broadcasted_iota(jnp.int32, (tq, tk), 0)
    col = lax.broadcasted_iota(jnp.int32, (tq, tk), 1)
    for d in range(nsub):
        k, v, bias = tile(i * nsub + d)
        mask = col + d * tk <= row
        c1 = _softmax_step(jnp.where(mask, _dot_nt(q1, k) + bias, NEG_INF), v, *c[:3])
        c2 = _softmax_step(jnp.where(mask, _dot_nt(q2, k) + bias, NEG_INF), v, *c[3:])
        c = c1 + c2
    lam = (jnp.exp(jnp.sum(lq1_ref[...] * lk1_ref[...], axis=-1, keepdims=True))
           - jnp.exp(jnp.sum(lq2_ref[...] * lk2_ref[...], axis=-1, keepdims=True)) + lambda_init)
    o = _softmax_out(c[1], c[2]) - lam * _softmax_out(c[4], c[5])
    o_ref[...] = (_rms_rows(o, sub_ref[...]) * (1.0 - lambda_init)).astype(o_ref.dtype)


def _diff_attn(q1, q2, k, v, slopes, lq1, lk1, lq2, lk2, sub, lambda_init, tq=256, tk=256):
    b, s, _ = q1.shape
    qs = pl.BlockSpec((None, tq, LANES), lambda bi, h, i: (bi, i, h))
    kvs = pl.BlockSpec((None, s, LANES), lambda bi, h, i: (bi, 0, h))
    vec = lambda a: pl.BlockSpec(a.shape, lambda bi, h, i: (0,) * a.ndim)
    return pl.pallas_call(
        functools.partial(_diff_attn_kernel, tq=tq, tk=tk, lambda_init=lambda_init),
        grid=(b, DIFF_HEADS, s // tq),
        in_specs=[qs, qs, kvs, kvs,
                  pl.BlockSpec((1, 1, LANES), lambda bi, h, i: (h, 0, 0)),
                  vec(lq1), vec(lk1), vec(lq2), vec(lk2), vec(sub)],
        out_specs=pl.BlockSpec((None, tq, DIFF_V), lambda bi, h, i: (bi, i, h)),
        out_shape=jax.ShapeDtypeStruct((b, s, DIFF_HEADS * DIFF_V), BF16),
        compiler_params=_params("parallel", "parallel", "arbitrary"),
        name="diff_attn",
    )(q1, q2, k, v, slopes, lq1, lk1, lq2, lk2, sub)


def _sb_attn_kernel(q_ref, k_ref, v_ref, tri_ref, on_ref, o_ref, *, t):
    i = pl.program_id(2)
    q = (q_ref[...].astype(F32) * (SB_D ** -0.5)).astype(BF16)
    tri = tri_ref[...]

    def step(j, carry, mask):
        run, acc = carry
        st = pl.multiple_of(j * t, t)
        k, v = k_ref[pl.ds(st, t), :], v_ref[pl.ds(st, t), :]
        z = _dot_nt(q, k)
        log_sig = jnp.minimum(z, 0.0) - jnp.log(1.0 + jnp.exp(-jnp.abs(z)))
        log_not = log_sig - z
        if mask is not None:
            log_not = jnp.where(mask, log_not, 0.0)
        hi = log_not.astype(BF16)
        lo = (log_not - hi.astype(F32)).astype(BF16)
        after = _dot(hi, tri) + _dot(lo, tri) + run
        a = jnp.exp(log_sig + after)
        if mask is not None:
            a = jnp.where(mask, a, 0.0)
        acc = acc + _dot(a.astype(BF16), v)
        run = run + jnp.sum(log_not, axis=-1, keepdims=True)
        return run, acc

    row = lax.broadcasted_iota(jnp.int32, (t, t), 0)
    col = lax.broadcasted_iota(jnp.int32, (t, t), 1)
    carry = step(i, (jnp.zeros((t, 1), F32), jnp.zeros((t, SB_D), F32)), col < row)
    _, acc = lax.fori_loop(0, i, lambda n, c: step(i - 1 - n, c, None), carry)
    o_ref[...] = _rms_rows(acc, on_ref[...]).astype(o_ref.dtype)


def _sb_attn(u, tri, on, t=256):
    b, s, _ = u.shape
    return pl.pallas_call(
        functools.partial(_sb_attn_kernel, t=t),
        grid=(b, SB_HEADS, s // t),
        in_specs=[pl.BlockSpec((None, t, SB_D), lambda bi, h, i: (bi, i, h)),
                  pl.BlockSpec((None, s, SB_D), lambda bi, h, i: (bi, 0, SB_HEADS + h)),
                  pl.BlockSpec((None, s, SB_D), lambda bi, h, i: (bi, 0, 2 * SB_HEADS + h)),
                  pl.BlockSpec((t, t), lambda bi, h, i: (0, 0)),
                  pl.BlockSpec((1, SB_D), lambda bi, h, i: (0, 0))],
        out_specs=pl.BlockSpec((None, t, SB_D), lambda bi, h, i: (bi, i, h)),
        out_shape=jax.ShapeDtypeStruct((b, s, SB_HEADS * SB_D), BF16),
        compiler_params=_params("parallel", "parallel", "arbitrary"),
        name="sb_attn",
    )(u, u, u, tri, on)


def _topk_mask(score, n):
    lane = lax.broadcasted_iota(jnp.int32, score.shape, 1).astype(F32)

    def body(_, c):
        work, sel = c
        mx = jnp.max(work, axis=-1, keepdims=True)
        idx = jnp.min(jnp.where(work == mx, lane, 1e9), axis=-1, keepdims=True)
        pick = lane == idx
        return jnp.where(pick, -jnp.inf, work), jnp.where(pick, 1.0, sel)

    _, sel = lax.fori_loop(0, n, body, (score, jnp.zeros_like(score)), unroll=True)
    return sel


def _nsa_attn_kernel(q_ref, g_ref, kc_ref, vc_ref, ks_ref, vs_ref, kw_ref, vw_ref, ov_ref, ex_ref,
                     on_ref, o_ref, selb_ref, *, t, n_sb, n_sel):
    i = pl.program_id(1)
    t0 = i * t
    ncp = kc_ref.shape[0]
    row = t0 + lax.broadcasted_iota(jnp.int32, (t, 1), 0)
    gates = g_ref[...]

    c_end = NSA_CMP_STRIDE * lax.broadcasted_iota(jnp.int32, (1, ncp), 1) + (NSA_CMP_LEN - 1)
    cmask = c_end <= row
    c_rel = (c_end - t0).astype(F32)
    kc, vc = kc_ref[...], vc_ref[...]
    pc_sum = jnp.zeros((t, ncp), F32)
    o_cmp = []
    for h in range(NSA_HEADS):
        qh = q_ref[:, h * LANES:(h + 1) * LANES]
        s = jnp.where(cmask, _dot_nt(qh, kc) + NSA_SLOPES[h] * c_rel, NEG_INF)
        m = jnp.max(s, axis=-1, keepdims=True)
        e = jnp.where(cmask, jnp.exp(s - m), 0.0)
        p = e * (1.0 / jnp.maximum(jnp.sum(e, axis=-1, keepdims=True), 1e-30))
        pc_sum = pc_sum + p
        o_cmp.append(_dot(p.astype(BF16), vc))
    ov = ov_ref[...]
    hi = pc_sum.astype(BF16)
    r1 = pc_sum - hi.astype(F32)
    mid = r1.astype(BF16)
    lo = (r1 - mid.astype(F32)).astype(BF16)
    imp = _dot(hi, ov) + _dot(mid, ov) + _dot(lo, ov)

    jb = lax.broadcasted_iota(jnp.int32, (1, LANES), 1)
    cur = lax.shift_right_logical(row, int(math.log2(NSA_SEL_BLOCK)))
    valid = (jb * NSA_SEL_BLOCK <= row) & (jb < n_sb)
    forced = (jb == 0) | (jb == cur) | (jb == cur - 1)
    score = jnp.where(valid, imp + jnp.where(forced, NSA_FORCE_BONUS, 0.0), NEG_INF)
    sel = jnp.where(score > 0.5 * NEG_INF, _topk_mask(score, n_sel), 0.0).astype(BF16)
    for jj in range(selb_ref.shape[0]):
        selb_ref[jj] = (_dot(sel, ex_ref[:, jj * t:(jj + 1) * t]) - 1.0) * (-NEG_INF)

    colrel = lax.broadcasted_iota(jnp.int32, (1, t), 1)
    rowl = lax.broadcasted_iota(jnp.int32, (t, t), 0)
    coll = lax.broadcasted_iota(jnp.int32, (t, t), 1)
    j_lo = jnp.maximum(i - NSA_WINDOW // t, 0)
    for h in range(NSA_HEADS):
        qh = q_ref[:, h * LANES:(h + 1) * LANES]
        slope = NSA_SLOPES[h]

        def scores(k_ref_, j):
            st = pl.multiple_of(j * t, t)
            bias = slope * (colrel + (j * t - t0)).astype(F32)
            return _dot_nt(qh, k_ref_[pl.ds(st, t), :]) + bias, st

        def sel_body(j, c):
            s, st = scores(ks_ref, j)
            return _softmax_step(s + selb_ref[j], vs_ref[pl.ds(st, t), :], *c)

        c = lax.fori_loop(0, i, sel_body, _softmax_init(t, NSA_D))
        s, st = scores(ks_ref, i)
        s = jnp.where(coll <= rowl, s + selb_ref[i], NEG_INF)
        _, l, acc = _softmax_step(s, vs_ref[pl.ds(st, t), :], *c)
        o_sel = _softmax_out(l, acc)

        def win_body(j, c):
            s, st = scores(kw_ref, j)
            d = rowl - coll + (t0 - j * t)
            s = jnp.where((d >= 0) & (d < NSA_WINDOW), s, NEG_INF)
            return _softmax_step(s, vw_ref[pl.ds(st, t), :], *c)

        _, l, acc = lax.fori_loop(j_lo, i + 1, win_body, _softmax_init(t, NSA_D))
        o_win = _softmax_out(l, acc)

        o = (gates[:, 3 * h:3 * h + 1] * o_cmp[h] + gates[:, 3 * h + 1:3 * h + 2] * o_sel
             + gates[:, 3 * h + 2:3 * h + 3] * o_win)
        o_ref[:, h * LANES:(h + 1) * LANES] = _rms_rows(o, on_ref[...]).astype(o_ref.dtype)


def _nsa_attn(q, gates, kc, vc, kv, ov, ex, on, t=128):
    b, s, _ = q.shape
    ncp = kc.shape[1]
    n_sb = s // NSA_SEL_BLOCK
    n_sel = min(NSA_N_SELECT, n_sb)
    cmp_spec = pl.BlockSpec((None, ncp, NSA_D), lambda bi, i: (bi, 0, 0))
    kv_spec = lambda n: pl.BlockSpec((None, s, NSA_D), lambda bi, i: (bi, 0, n))
    full = lambda a: pl.BlockSpec(a.shape, lambda bi, i: (0,) * a.ndim)
    return pl.pallas_call(
        functools.partial(_nsa_attn_kernel, t=t, n_sb=n_sb, n_sel=n_sel),
        grid=(b, s // t),
        in_specs=[pl.BlockSpec((None, t, NSA_HEADS * NSA_D), lambda bi, i: (bi, i, 0)),
                  pl.BlockSpec((None, t, LANES), lambda bi, i: (bi, i, 0)),
                  cmp_spec, cmp_spec, kv_spec(0), kv_spec(1), kv_spec(2), kv_spec(3),
                  full(ov), full(ex), full(on)],
        out_specs=pl.BlockSpec((None, t, NSA_HEADS * NSA_D), lambda bi, i: (bi, i, 0)),
        out_shape=jax.ShapeDtypeStruct((b, s, NSA_HEADS * NSA_D), BF16),
        scratch_shapes=[pltpu.VMEM((s // t, t, t), F32)],
        compiler_params=_params("parallel", "arbitrary"),
        name="nsa_attn",
    )(q, gates, kc, vc, kv, kv, kv, kv, ov, ex, on)


def _pad_cols(w, width):
    return jnp.pad(w, ((0, 0), (0, width - w.shape[1])))


def _swap_halves(w):
    half = w.shape[-1] // 2
    return jnp.concatenate([w[..., half:], w[..., :half]], axis=-1)


def _row(v):
    return v.reshape(1, -1).astype(F32)


def _nsa_constants(s, t):
    ncp = s // NSA_CMP_STRIDE
    n_sb = s // NSA_SEL_BLOCK
    c_start = NSA_CMP_STRIDE * np.arange(ncp)[:, None]
    sel_start = NSA_SEL_BLOCK * np.arange(LANES)[None, :]
    overlap = ((c_start < sel_start + NSA_SEL_BLOCK) & (c_start + NSA_CMP_LEN > sel_start)
               & (np.arange(LANES)[None, :] < n_sb))
    expand = (np.arange(s)[None, :] // NSA_SEL_BLOCK) == np.arange(LANES)[:, None]
    return jnp.asarray(overlap, BF16), jnp.asarray(expand, BF16)


def kernel(x, ffn1_norm, ffn1_w_gate, ffn1_w_up, ffn1_w_down, mix_norm, w_in, mla_cq_norm, mla_ckv_norm, mla_w_uq, mla_w_ukv, mla_qn_norm, mla_qr_norm, mla_kn_norm, mla_kr_norm, mla_o_norm, diff_q_norm, diff_k_norm, diff_lq1, diff_lk1, diff_lq2, diff_lk2, diff_subln, nsa_q_norm, nsa_pe_k, nsa_w_ck, nsa_pe_v, nsa_w_cv, nsa_kc_norm, nsa_ks_norm, nsa_kw_norm, nsa_o_norm, sb_o_norm, w_out, ffn2_norm, ffn2_w_gate, ffn2_w_up, ffn2_w_down):
    b, s, d = x.shape
    m = b * s
    depth = w_in.shape[0]

    pos = jnp.arange(s, dtype=F32)
    inv_freq = ROPE_THETA ** (-jnp.arange(0, MLA_ROPE, 2, dtype=F32) / MLA_ROPE)
    ang = pos[:, None] * inv_freq[None, :]
    cos, sin = jnp.cos(ang), jnp.sin(ang)
    cos2 = _pad_cols(jnp.concatenate([cos, cos], axis=-1), LANES)
    sin2 = _pad_cols(jnp.concatenate([-sin, sin], axis=-1), LANES)
    diff_slopes = jnp.broadcast_to(jnp.asarray(DIFF_SLOPES, F32)[:, None, None], (DIFF_HEADS, 1, LANES))
    sb_t = 256
    tri = jnp.asarray(np.arange(sb_t)[:, None] > np.arange(sb_t)[None, :], BF16)
    overlap, expand = _nsa_constants(s, 128)

    h = x.reshape(m, d)
    for l in range(depth):
        lambda_init = 0.8 - 0.6 * math.exp(-0.3 * l)
        h = _ffn(h, ffn1_norm[l], ffn1_w_gate[l].astype(BF16), ffn1_w_up[l].astype(BF16),
                 ffn1_w_down[l].astype(BF16))

        g_mix = _row(mix_norm[l])
        wl = w_in[l]
        o0 = 0
        w_mla, o0 = wl[:, o0:o0 + W_MLA], o0 + W_MLA
        w_diff, o0 = wl[:, o0:o0 + W_DIFF], o0 + W_DIFF
        w_nsa, o0 = wl[:, o0:o0 + W_NSA], o0 + W_NSA
        w_sb = wl[:, o0:o0 + W_SB]

        w_kr = w_mla[:, MLA_Q_RANK + MLA_KV_RANK:]
        w_mla_p = jnp.concatenate([w_mla[:, :MLA_Q_RANK + MLA_KV_RANK], _pad_cols(w_kr, LANES),
                                   _pad_cols(_swap_halves(w_kr), LANES)], axis=1).astype(BF16)
        wuq = mla_w_uq[l].reshape(MLA_Q_RANK, MLA_HEADS, MLA_NOPE + MLA_ROPE)
        wuq_r = wuq[..., MLA_NOPE:]
        pad3 = lambda a: jnp.pad(a, ((0, 0), (0, 0), (0, LANES - a.shape[-1])))
        wuq_p = jnp.concatenate([wuq[..., :MLA_NOPE], pad3(wuq_r), pad3(_swap_halves(wuq_r))],
                                axis=-1).reshape(MLA_Q_RANK, MLA_HEADS * 3 * LANES).astype(BF16)
        padr = lambda v: _pad_cols(_row(v), LANES)
        q_mla, k_mla, v_mla = _mla_prep(
            h.reshape(b, s, d), g_mix, w_mla_p, _row(mla_cq_norm[l]), _row(mla_ckv_norm[l]), wuq_p,
            mla_w_ukv[l].astype(BF16), _row(mla_qn_norm[l]), padr(mla_qr_norm[l]),
            padr(_swap_halves(mla_qr_norm[l])), _row(mla_kn_norm[l]), padr(mla_kr_norm[l]),
            padr(_swap_halves(mla_kr_norm[l])), cos2, sin2)
        o_a = _mla_attn(q_mla, k_mla, v_mla, _row(mla_o_norm[l]))

        two = lambda v: jnp.concatenate([_row(v), _row(v)], axis=1)
        q1, q2, k_d, v_d = _diff_prep(h, g_mix, w_diff.astype(BF16), two(diff_q_norm[l]), two(diff_k_norm[l]))
        r3 = lambda a: a.reshape(b, s, -1)
        o_b = _diff_attn(r3(q1), r3(q2), r3(k_d), r3(v_d), diff_slopes, _row(diff_lq1[l]), _row(diff_lk1[l]),
                         _row(diff_lq2[l]), _row(diff_lk2[l]), _row(diff_subln[l]), lambda_init)

        w_nsa_p = _pad_cols(w_nsa, NSA_HEADS * LANES + 7 * LANES).astype(BF16)
        q_n, kv_n, raw_n, gates = _nsa_prep(h, g_mix, w_nsa_p, _row(nsa_q_norm[l]), _row(nsa_ks_norm[l]),
                                            _row(nsa_kw_norm[l]))
        chunks = lambda z: z.reshape(b, s // NSA_CMP_STRIDE, NSA_CMP_STRIDE * NSA_D)
        kc, vc = _nsa_compress(chunks(raw_n[:, :LANES]), chunks(raw_n[:, LANES:]),
                               nsa_pe_k[l].reshape(1, -1), nsa_pe_v[l].reshape(1, -1),
                               nsa_w_ck[l].astype(BF16), nsa_w_cv[l].astype(BF16), _row(nsa_kc_norm[l]))
        o_c = _nsa_attn(r3(q_n), r3(gates), kc, vc, r3(kv_n), overlap, expand, _row(nsa_o_norm[l]))

        u_sb = _sb_prep(h, mix_norm[l], w_sb.astype(BF16))
        o_d = _sb_attn(r3(u_sb), tri, _row(sb_o_norm[l]), t=sb_t)

        flat = lambda a: a.reshape(m, -1)
        h = _out_proj([flat(o_a), flat(o_b), flat(o_c), flat(o_d)], w_out[l].astype(BF16), h)
        h = _ffn(h, ffn2_norm[l], ffn2_w_gate[l].astype(BF16), ffn2_w_up[l].astype(BF16),
                 ffn2_w_down[l].astype(BF16))
    return h.reshape(b, s, d)
```

```python
import functools
import math

import numpy as np
import jax
import jax.numpy as jnp
from jax import lax
from jax.experimental import pallas as pl
from jax.experimental.pallas import tpu as pltpu

F32 = jnp.float32
BF16 = jnp.bfloat16

RMS_EPS = 1e-6
NEG_INF = -1e30
LANES = 128
VMEM_LIMIT = 56 * 1024 * 1024

MLA_HEADS, MLA_Q_RANK, MLA_KV_RANK, MLA_NOPE, MLA_ROPE, MLA_V = 4, 512, 256, 128, 64, 128
ROPE_THETA = 10000.0
DIFF_HEADS, DIFF_QK, DIFF_V = 4, 64, 128
NSA_HEADS, NSA_D = 4, 128
NSA_CMP_LEN, NSA_CMP_STRIDE, NSA_SEL_BLOCK, NSA_N_SELECT, NSA_WINDOW = 32, 16, 64, 16, 512
NSA_FORCE_BONUS = 1e4
SB_HEADS, SB_D = 4, 128
N_ALIBI_HEADS = DIFF_HEADS + NSA_HEADS
ALIBI_SLOPES = [2.0 ** (-8.0 * k / N_ALIBI_HEADS) for k in range(1, N_ALIBI_HEADS + 1)]
DIFF_SLOPES, NSA_SLOPES = ALIBI_SLOPES[0::2], ALIBI_SLOPES[1::2]

W_MLA = MLA_Q_RANK + MLA_KV_RANK + MLA_ROPE
W_DIFF = DIFF_HEADS * (2 * DIFF_QK + 2 * DIFF_QK + DIFF_V)
W_NSA = NSA_HEADS * NSA_D + 6 * NSA_D + NSA_HEADS * 3
W_SB = 3 * SB_HEADS * SB_D


def _dot(a, b):
    return jnp.dot(a, b, preferred_element_type=F32)


def _dot_nt(a, b):
    return lax.dot_general(a, b, (((1,), (1,)), ((), ())), preferred_element_type=F32)


def _rms_rows(x, g):
    ms = jnp.mean(x * x, axis=-1, keepdims=True)
    return x * lax.rsqrt(ms + RMS_EPS) * g


def _params(*sem):
    return pltpu.CompilerParams(dimension_semantics=sem, vmem_limit_bytes=VMEM_LIMIT)


def _ffn_kernel(x_ref, g_ref, wg_ref, wu_ref, wd_ref, o_ref, xn_ref, acc_ref, *, nf):
    f = pl.program_id(1)

    @pl.when(f == 0)
    def _():
        xn_ref[...] = _rms_rows(x_ref[...], g_ref[...]).astype(BF16)
        acc_ref[...] = jnp.zeros_like(acc_ref)

    xn = xn_ref[...]
    gate = _dot(xn, wg_ref[...])
    up = _dot(xn, wu_ref[...])
    mid = (gate * jax.nn.sigmoid(gate) * up).astype(BF16)
    acc_ref[...] += _dot(mid, wd_ref[...])

    @pl.when(f == nf - 1)
    def _():
        o_ref[...] = x_ref[...] + 0.5 * acc_ref[...]


def _ffn(h, g, wg, wu, wd, tm=512, tf=512):
    m, d = h.shape
    f = wg.shape[1]
    return pl.pallas_call(
        functools.partial(_ffn_kernel, nf=f // tf),
        grid=(m // tm, f // tf),
        in_specs=[
            pl.BlockSpec((tm, d), lambda i, j: (i, 0)),
            pl.BlockSpec((1, d), lambda i, j: (0, 0)),
            pl.BlockSpec((d, tf), lambda i, j: (0, j)),
            pl.BlockSpec((d, tf), lambda i, j: (0, j)),
            pl.BlockSpec((tf, d), lambda i, j: (j, 0)),
        ],
        out_specs=pl.BlockSpec((tm, d), lambda i, j: (i, 0)),
        out_shape=jax.ShapeDtypeStruct((m, d), F32),
        scratch_shapes=[pltpu.VMEM((tm, d), BF16), pltpu.VMEM((tm, d), F32)],
        compiler_params=_params("parallel", "arbitrary"),
        name="ffn",
    )(h, g.reshape(1, d), wg, wu, wd)


def _out_proj_kernel(a0_ref, a1_ref, a2_ref, a3_ref, w_ref, r_ref, o_ref):
    acc = r_ref[...]
    for n, a_ref in enumerate((a0_ref, a1_ref, a2_ref, a3_ref)):
        kw = a_ref.shape[1]
        acc = acc + _dot(a_ref[...], w_ref[n * kw:(n + 1) * kw, :])
    o_ref[...] = acc


def _out_proj(parts, w, res, tm=512, tn=1024):
    m, d = res.shape
    kw = parts[0].shape[1]
    a_spec = pl.BlockSpec((tm, kw), lambda i, j: (i, 0))
    return pl.pallas_call(
        _out_proj_kernel,
        grid=(m // tm, d // tn),
        in_specs=[a_spec, a_spec, a_spec, a_spec,
                  pl.BlockSpec((w.shape[0], tn), lambda i, j: (0, j)),
                  pl.BlockSpec((tm, tn), lambda i, j: (i, j))],
        out_specs=pl.BlockSpec((tm, tn), lambda i, j: (i, j)),
        out_shape=jax.ShapeDtypeStruct((m, d), F32),
        compiler_params=_params("parallel", "arbitrary"),
        name="out_proj",
    )(*parts, w, res)


def _sb_prep_kernel(x_ref, g_ref, w_ref, o_ref):
    xn = _rms_rows(x_ref[...], g_ref[...]).astype(BF16)
    o_ref[...] = _dot(xn, w_ref[...]).astype(o_ref.dtype)


def _sb_prep(h, g, w, tm=512):
    m, d = h.shape
    n = w.shape[1]
    return pl.pallas_call(
        _sb_prep_kernel,
        grid=(m // tm,),
        in_specs=[pl.BlockSpec((tm, d), lambda i: (i, 0)),
                  pl.BlockSpec((1, d), lambda i: (0, 0)),
                  pl.BlockSpec((d, n), lambda i: (0, 0))],
        out_specs=pl.BlockSpec((tm, n), lambda i: (i, 0)),
        out_shape=jax.ShapeDtypeStruct((m, n), BF16),
        compiler_params=_params("parallel"),
        name="sb_prep",
    )(h, g.reshape(1, d), w)


def _rope_rows(x, x_sw, g, g_sw, cos2, sin2):
    ms = jnp.sum(x * x, axis=-1, keepdims=True) * (1.0 / MLA_ROPE)
    r = lax.rsqrt(ms + RMS_EPS)
    return (x * r * g) * cos2 + (x_sw * r * g_sw) * sin2


def _mla_prep_kernel(x_ref, g_ref, w_ref, cqn_ref, ckvn_ref, wuq_ref, wukv_ref, qn_ref, qr_ref,
                     qrs_ref, kn_ref, kr_ref, krs_ref, cos_ref, sin_ref, q_out, k_out, v_out):
    xn = _rms_rows(x_ref[...], g_ref[...]).astype(BF16)
    u = _dot(xn, w_ref[...])
    c_q = u[:, :MLA_Q_RANK]
    c_kv = u[:, MLA_Q_RANK:MLA_Q_RANK + MLA_KV_RANK]
    kr = u[:, MLA_Q_RANK + MLA_KV_RANK:MLA_Q_RANK + MLA_KV_RANK + LANES]
    kr_sw = u[:, MLA_Q_RANK + MLA_KV_RANK + LANES:]
    cos2, sin2 = cos_ref[...], sin_ref[...]
    qall = _dot(_rms_rows(c_q, cqn_ref[...]).astype(BF16), wuq_ref[...])
    kvall = _dot(_rms_rows(c_kv, ckvn_ref[...]).astype(BF16), wukv_ref[...])
    k_rot = _rope_rows(kr, kr_sw, kr_ref[...], krs_ref[...], cos2, sin2).astype(BF16)
    scale = (MLA_NOPE + MLA_ROPE) ** -0.5
    for h in range(MLA_HEADS):
        qh = qall[:, h * 3 * LANES:(h + 1) * 3 * LANES]
        q_nope = _rms_rows(qh[:, :LANES], qn_ref[...]) * scale
        q_rot = _rope_rows(qh[:, LANES:2 * LANES], qh[:, 2 * LANES:], qr_ref[...], qrs_ref[...],
                           cos2, sin2) * scale
        q_out[:, h * 2 * LANES:h * 2 * LANES + LANES] = q_nope.astype(BF16)
        q_out[:, h * 2 * LANES + LANES:(h + 1) * 2 * LANES] = q_rot.astype(BF16)
        kvh = kvall[:, h * 2 * LANES:(h + 1) * 2 * LANES]
        k_out[:, h * 2 * LANES:h * 2 * LANES + LANES] = _rms_rows(kvh[:, :LANES], kn_ref[...]).astype(BF16)
        k_out[:, h * 2 * LANES + LANES:(h + 1) * 2 * LANES] = k_rot
        v_out[:, h * LANES:(h + 1) * LANES] = kvh[:, LANES:].astype(BF16)


def _mla_prep(h3, g, w, cqn, ckvn, wuq, wukv, qn, qr, qrs, kn, kr, krs, cos2, sin2, tm=512):
    b, s, d = h3.shape
    full = lambda a: pl.BlockSpec(a.shape, lambda bi, i: (0,) * a.ndim)
    smalls = [g, w, cqn, ckvn, wuq, wukv, qn, qr, qrs, kn, kr, krs]
    tab = pl.BlockSpec((tm, LANES), lambda bi, i: (i, 0))
    out = lambda c: pl.BlockSpec((None, tm, c), lambda bi, i: (bi, i, 0))
    return pl.pallas_call(
        _mla_prep_kernel,
        grid=(b, s // tm),
        in_specs=[pl.BlockSpec((None, tm, d), lambda bi, i: (bi, i, 0))] + [full(a) for a in smalls] + [tab, tab],
        out_specs=[out(MLA_HEADS * 2 * LANES), out(MLA_HEADS * 2 * LANES), out(MLA_HEADS * MLA_V)],
        out_shape=[jax.ShapeDtypeStruct((b, s, MLA_HEADS * 2 * LANES), BF16),
                   jax.ShapeDtypeStruct((b, s, MLA_HEADS * 2 * LANES), BF16),
                   jax.ShapeDtypeStruct((b, s, MLA_HEADS * MLA_V), BF16)],
        compiler_params=_params("parallel", "parallel"),
        name="mla_prep",
    )(h3, *smalls, cos2, sin2)


def _diff_prep_kernel(x_ref, g_ref, w_ref, qn_ref, kn_ref, q1_out, q2_out, k_out, v_out):
    xn = _rms_rows(x_ref[...], g_ref[...]).astype(BF16)
    u = _dot(xn, w_ref[...])
    width = DIFF_HEADS * LANES
    lane = lax.broadcasted_iota(jnp.int32, (1, LANES), 1)
    lo = lane < DIFF_QK
    scale = DIFF_QK ** -0.5

    def half_rms(x, g2):
        sq = x * x
        s_lo = jnp.sum(jnp.where(lo, sq, 0.0), axis=-1, keepdims=True)
        s_hi = jnp.sum(jnp.where(lo, 0.0, sq), axis=-1, keepdims=True)
        r = jnp.where(lo, lax.rsqrt(s_lo * (1.0 / DIFF_QK) + RMS_EPS),
                      lax.rsqrt(s_hi * (1.0 / DIFF_QK) + RMS_EPS))
        return x * r * g2

    for h in range(DIFF_HEADS):
        sl = slice(h * LANES, (h + 1) * LANES)
        qn = half_rms(u[:, sl], qn_ref[...]) * scale
        q1_out[:, sl] = jnp.where(lo, qn, 0.0).astype(BF16)
        q2_out[:, sl] = jnp.where(lo, 0.0, qn).astype(BF16)
        k_out[:, sl] = half_rms(u[:, width + h * LANES:width + (h + 1) * LANES], kn_ref[...]).astype(BF16)
    v_out[...] = u[:, 2 * width:].astype(BF16)


def _diff_prep(h, g, w, qn2, kn2, tm=512):
    m, d = h.shape
    width = DIFF_HEADS * LANES
    full = lambda a: pl.BlockSpec(a.shape, lambda i: (0,) * a.ndim)
    out = pl.BlockSpec((tm, width), lambda i: (i, 0))
    return pl.pallas_call(
        _diff_prep_kernel,
        grid=(m // tm,),
        in_specs=[pl.BlockSpec((tm, d), lambda i: (i, 0)), full(g), full(w), full(qn2), full(kn2)],
        out_specs=[out] * 4,
        out_shape=[jax.ShapeDtypeStruct((m, width), BF16)] * 4,
        compiler_params=_params("parallel"),
        name="diff_prep",
    )(h, g, w, qn2, kn2)


def _nsa_prep_kernel(x_ref, g_ref, w_ref, qn_ref, ksn_ref, kwn_ref, q_out, kv_out, raw_out, gate_out):
    xn = _rms_rows(x_ref[...], g_ref[...]).astype(BF16)
    u = _dot(xn, w_ref[...])
    scale = NSA_D ** -0.5
    for h in range(NSA_HEADS):
        sl = slice(h * LANES, (h + 1) * LANES)
        q_out[:, sl] = (_rms_rows(u[:, sl], qn_ref[...]) * scale).astype(BF16)
    base = NSA_HEADS * LANES
    piece = lambda n: u[:, base + n * LANES:base + (n + 1) * LANES]
    raw_out[:, :LANES] = piece(0)
    raw_out[:, LANES:] = piece(1)
    kv_out[:, 0 * LANES:1 * LANES] = _rms_rows(piece(2), ksn_ref[...]).astype(BF16)
    kv_out[:, 1 * LANES:2 * LANES] = piece(3).astype(BF16)
    kv_out[:, 2 * LANES:3 * LANES] = _rms_rows(piece(4), kwn_ref[...]).astype(BF16)
    kv_out[:, 3 * LANES:4 * LANES] = piece(5).astype(BF16)
    gate_out[...] = jax.nn.sigmoid(piece(6))


def _nsa_prep(h, g, w, qn, ksn, kwn, tm=512):
    m, d = h.shape
    full = lambda a: pl.BlockSpec(a.shape, lambda i: (0,) * a.ndim)
    out = lambda c: pl.BlockSpec((tm, c), lambda i: (i, 0))
    return pl.pallas_call(
        _nsa_prep_kernel,
        grid=(m // tm,),
        in_specs=[pl.BlockSpec((tm, d), lambda i: (i, 0)), full(g), full(w), full(qn), full(ksn), full(kwn)],
        out_specs=[out(NSA_HEADS * LANES), out(4 * LANES), out(2 * LANES), out(LANES)],
        out_shape=[jax.ShapeDtypeStruct((m, NSA_HEADS * LANES), BF16),
                   jax.ShapeDtypeStruct((m, 4 * LANES), BF16),
                   jax.ShapeDtypeStruct((m, 2 * LANES), F32),
                   jax.ShapeDtypeStruct((m, LANES), F32)],
        compiler_params=_params("parallel"),
        name="nsa_prep",
    )(h, g, w, qn, ksn, kwn)


def _nsa_compress_kernel(zk_ref, zv_ref, pek_ref, pev_ref, wk_ref, wv_ref, kcn_ref, kc_out, vc_out):
    half = NSA_CMP_STRIDE * NSA_D
    nchunk = zk_ref.shape[0]

    def compress(z, pe_ref, w_ref):
        first = _dot((z + pe_ref[:, :half]).astype(BF16), w_ref[:half, :])
        second = _dot((z + pe_ref[:, half:]).astype(BF16), w_ref[half:, :])
        return first + pltpu.roll(second, shift=nchunk - 1, axis=0)

    kc_out[...] = _rms_rows(compress(zk_ref[...], pek_ref, wk_ref), kcn_ref[...]).astype(BF16)
    vc_out[...] = compress(zv_ref[...], pev_ref, wv_ref).astype(BF16)


def _nsa_compress(zk, zv, pek, pev, wk, wv, kcn):
    b, nchunk, width = zk.shape
    full = lambda a: pl.BlockSpec(a.shape, lambda bi: (0,) * a.ndim)
    z_spec = pl.BlockSpec((None, nchunk, width), lambda bi: (bi, 0, 0))
    o_spec = pl.BlockSpec((None, nchunk, NSA_D), lambda bi: (bi, 0, 0))
    return pl.pallas_call(
        _nsa_compress_kernel,
        grid=(b,),
        in_specs=[z_spec, z_spec, full(pek), full(pev), full(wk), full(wv), full(kcn)],
        out_specs=[o_spec, o_spec],
        out_shape=[jax.ShapeDtypeStruct((b, nchunk, NSA_D), BF16)] * 2,
        compiler_params=_params("parallel"),
        name="nsa_compress",
    )(zk, zv, pek, pev, wk, wv, kcn)


def _softmax_step(s, v, m, l, acc):
    m_new = jnp.maximum(m, jnp.max(s, axis=-1, keepdims=True))
    alpha = jnp.exp(m - m_new)
    p = jnp.exp(s - m_new)
    l = alpha * l + jnp.sum(p, axis=-1, keepdims=True)
    acc = alpha * acc + _dot(p.astype(BF16), v)
    return m_new, l, acc


def _softmax_init(tq, dv):
    return (jnp.full((tq, 1), NEG_INF, F32), jnp.zeros((tq, 1), F32), jnp.zeros((tq, dv), F32))


def _softmax_out(l, acc):
    return acc * (1.0 / jnp.maximum(l, 1e-30))


def _mla_attn_kernel(q_ref, k_ref, v_ref, on_ref, o_ref, *, t):
    i = pl.program_id(1)
    dq = 2 * LANES
    qs = [q_ref[:, h * dq:(h + 1) * dq] for h in range(MLA_HEADS)]

    def step(j, carry, mask):
        st = pl.multiple_of(j * t, t)
        out = ()
        for h in range(MLA_HEADS):
            s = _dot_nt(qs[h], k_ref[pl.ds(st, t), h * dq:(h + 1) * dq])
            if mask is not None:
                s = jnp.where(mask, s, NEG_INF)
            out += _softmax_step(s, v_ref[pl.ds(st, t), h * MLA_V:(h + 1) * MLA_V], *carry[3 * h:3 * h + 3])
        return out

    c = lax.fori_loop(0, i, lambda j, c: step(j, c, None), _softmax_init(t, MLA_V) * MLA_HEADS)
    row = lax.broadcasted_iota(jnp.int32, (t, t), 0)
    col = lax.broadcasted_iota(jnp.int32, (t, t), 1)
    c = step(i, c, col <= row)
    for h in range(MLA_HEADS):
        o = _softmax_out(c[3 * h + 1], c[3 * h + 2])
        o_ref[:, h * MLA_V:(h + 1) * MLA_V] = _rms_rows(o, on_ref[...]).astype(o_ref.dtype)


def _mla_attn(q, k, v, on, t=256):
    b, s, _ = q.shape
    return pl.pallas_call(
        functools.partial(_mla_attn_kernel, t=t),
        grid=(b, s // t),
        in_specs=[pl.BlockSpec((None, t, q.shape[2]), lambda bi, i: (bi, i, 0)),
                  pl.BlockSpec((None, s, k.shape[2]), lambda bi, i: (bi, 0, 0)),
                  pl.BlockSpec((None, s, v.shape[2]), lambda bi, i: (bi, 0, 0)),
                  pl.BlockSpec((1, MLA_V), lambda bi, i: (0, 0))],
        out_specs=pl.BlockSpec((None, t, MLA_HEADS * MLA_V), lambda bi, i: (bi, i, 0)),
        out_shape=jax.ShapeDtypeStruct((b, s, MLA_HEADS * MLA_V), BF16),
        compiler_params=_params("parallel", "arbitrary"),
        name="mla_attn",
    )(q, k, v, on)


def _diff_attn_kernel(q1_ref, q2_ref, k_ref, v_ref, lq1_ref, lk1_ref, lq2_ref, lk2_ref,
                      sub_ref, o_ref, *, t, lambda_init):
    i = pl.program_id(1)
    t0 = i * t
    colrel = lax.broadcasted_iota(jnp.int32, (1, t), 1)
    qs = [jnp.concatenate([q1_ref[:, h * LANES:(h + 1) * LANES], q2_ref[:, h * LANES:(h + 1) * LANES]], axis=0)
          for h in range(DIFF_HEADS)]

    def step(j, carry, mask):
        st = pl.multiple_of(j * t, t)
        rel = (colrel + (j * t - t0)).astype(F32)
        out = ()
        for h in range(DIFF_HEADS):
            sl = slice(h * LANES, (h + 1) * LANES)
            s = _dot_nt(qs[h], k_ref[pl.ds(st, t), sl]) + DIFF_SLOPES[h] * rel
            if mask is not None:
                s = jnp.where(mask, s, NEG_INF)
            out += _softmax_step(s, v_ref[pl.ds(st, t), sl], *carry[3 * h:3 * h + 3])
        return out

    c = lax.fori_loop(0, i, lambda j, c: step(j, c, None), _softmax_init(2 * t, DIFF_V) * DIFF_HEADS)
    row = lax.broadcasted_iota(jnp.int32, (t, t), 0)
    col = lax.broadcasted_iota(jnp.int32, (t, t), 1)
    causal = col <= row
    c = step(i, c, jnp.concatenate([causal, causal], axis=0))
    lam = (jnp.exp(jnp.sum(lq1_ref[...] * lk1_ref[...], axis=-1, keepdims=True))
           - jnp.exp(jnp.sum(lq2_ref[...] * lk2_ref[...], axis=-1, keepdims=True)) + lambda_init)
    for h in range(DIFF_HEADS):
        o12 = _softmax_out(c[3 * h + 1], c[3 * h + 2])
        o = o12[:t] - lam * o12[t:]
        o_ref[:, h * DIFF_V:(h + 1) * DIFF_V] = (_rms_rows(o, sub_ref[...]) * (1.0 - lambda_init)).astype(o_ref.dtype)


def _diff_attn(q1, q2, k, v, lq1, lk1, lq2, lk2, sub, lambda_init, t=256):
    b, s, w = q1.shape
    qs = pl.BlockSpec((None, t, w), lambda bi, i: (bi, i, 0))
    kvs = pl.BlockSpec((None, s, w), lambda bi, i: (bi, 0, 0))
    vec = lambda a: pl.BlockSpec(a.shape, lambda bi, i: (0,) * a.ndim)
    return pl.pallas_call(
        functools.partial(_diff_attn_kernel, t=t, lambda_init=lambda_init),
        grid=(b, s // t),
        in_specs=[qs, qs, kvs, kvs, vec(lq1), vec(lk1), vec(lq2), vec(lk2), vec(sub)],
        out_specs=pl.BlockSpec((None, t, DIFF_HEADS * DIFF_V), lambda bi, i: (bi, i, 0)),
        out_shape=jax.ShapeDtypeStruct((b, s, DIFF_HEADS * DIFF_V), BF16),
        compiler_params=_params("parallel", "arbitrary"),
        name="diff_attn",
    )(q1, q2, k, v, lq1, lk1, lq2, lk2, sub)


def _sb_attn_kernel(q_ref, k_ref, v_ref, tri_ref, on_ref, o_ref, *, t):
    i = pl.program_id(1)
    tri = tri_ref[...]
    qs = [(q_ref[:, h * SB_D:(h + 1) * SB_D].astype(F32) * (SB_D ** -0.5)).astype(BF16) for h in range(SB_HEADS)]

    def step(j, carry, mask):
        st = pl.multiple_of(j * t, t)
        out = ()
        for h in range(SB_HEADS):
            run, acc = carry[2 * h:2 * h + 2]
            sl = slice(h * SB_D, (h + 1) * SB_D)
            z = _dot_nt(qs[h], k_ref[pl.ds(st, t), sl])
            log_sig = jnp.minimum(z, 0.0) - jnp.log(1.0 + jnp.exp(-jnp.abs(z)))
            log_not = log_sig - z
            if mask is not None:
                log_not = jnp.where(mask, log_not, 0.0)
            hi = log_not.astype(BF16)
            lo = (log_not - hi.astype(F32)).astype(BF16)
            after = _dot(hi, tri) + _dot(lo, tri) + run
            a = jnp.exp(log_sig + after)
            if mask is not None:
                a = jnp.where(mask, a, 0.0)
            acc = acc + _dot(a.astype(BF16), v_ref[pl.ds(st, t), sl])
            run = run + jnp.sum(log_not, axis=-1, keepdims=True)
            out += (run, acc)
        return out

    row = lax.broadcasted_iota(jnp.int32, (t, t), 0)
    col = lax.broadcasted_iota(jnp.int32, (t, t), 1)
    carry = step(i, (jnp.zeros((t, 1), F32), jnp.zeros((t, SB_D), F32)) * SB_HEADS, col < row)
    carry = lax.fori_loop(0, i, lambda n, c: step(i - 1 - n, c, None), carry)
    for h in range(SB_HEADS):
        o_ref[:, h * SB_D:(h + 1) * SB_D] = _rms_rows(carry[2 * h + 1], on_ref[...]).astype(o_ref.dtype)


def _sb_attn(u, tri, on, t=256):
    b, s, _ = u.shape
    w = SB_HEADS * SB_D
    return pl.pallas_call(
        functools.partial(_sb_attn_kernel, t=t),
        grid=(b, s // t),
        in_specs=[pl.BlockSpec((None, t, w), lambda bi, i: (bi, i, 0)),
                  pl.BlockSpec((None, s, w), lambda bi, i: (bi, 0, 1)),
                  pl.BlockSpec((None, s, w), lambda bi, i: (bi, 0, 2)),
                  pl.BlockSpec((t, t), lambda bi, i: (0, 0)),
                  pl.BlockSpec((1, SB_D), lambda bi, i: (0, 0))],
        out_specs=pl.BlockSpec((None, t, w), lambda bi, i: (bi, i, 0)),
        out_shape=jax.ShapeDtypeStruct((b, s, w), BF16),
        compiler_params=_params("parallel", "arbitrary"),
        name="sb_attn",
    )(u, u, u, tri, on)


def _topk_mask(score, n):
    lane = lax.broadcasted_iota(jnp.int32, score.shape, 1).astype(F32)

    def body(_, c):
        work, sel = c
        mx = jnp.max(work, axis=-1, keepdims=True)
        idx = jnp.min(jnp.where(work == mx, lane, 1e9), axis=-1, keepdims=True)
        pick = lane == idx
        return jnp.where(pick, -jnp.inf, work), jnp.where(pick, 1.0, sel)

    _, sel = lax.fori_loop(0, n, body, (score, jnp.zeros_like(score)), unroll=True)
    return sel


def _softmax_exact(s, mask):
    m = jnp.max(s, axis=-1, keepdims=True)
    e = jnp.where(mask, jnp.exp(s - m), 0.0)
    return e * (1.0 / jnp.maximum(jnp.sum(e, axis=-1, keepdims=True), 1e-30))


def _nsa_attn_kernel(q_ref, g_ref, kc_ref, vc_ref, ks_ref, vs_ref, kw_ref, vw_ref, ov_ref, ex_ref,
                     on_ref, o_ref, selb_ref, *, t, tk, n_sb, n_sel):
    i = pl.program_id(1)
    t0 = i * t
    ncp = kc_ref.shape[0]
    nh = NSA_HEADS
    stack = lambda parts: jnp.concatenate(parts, axis=0)
    q4 = stack([q_ref[:, h * LANES:(h + 1) * LANES] for h in range(nh)])
    row1 = t0 + lax.broadcasted_iota(jnp.int32, (t, 1), 0)
    row4 = stack([row1] * nh)
    slope4 = stack([jnp.full((t, 1), NSA_SLOPES[h], F32) for h in range(nh)])
    gates = g_ref[...]

    c_end = NSA_CMP_STRIDE * lax.broadcasted_iota(jnp.int32, (1, ncp), 1) + (NSA_CMP_LEN - 1)
    cmask = c_end <= row4
    s = jnp.where(cmask, _dot_nt(q4, kc_ref[...]) + slope4 * (c_end - t0).astype(F32), NEG_INF)
    p = _softmax_exact(s, cmask)
    o_cmp = _dot(p.astype(BF16), vc_ref[...])
    pc_sum = p[0:t]
    for h in range(1, nh):
        pc_sum = pc_sum + p[h * t:(h + 1) * t]
    ov = ov_ref[...]
    hi = pc_sum.astype(BF16)
    r1 = pc_sum - hi.astype(F32)
    mid = r1.astype(BF16)
    lo = (r1 - mid.astype(F32)).astype(BF16)
    imp = _dot(hi, ov) + _dot(mid, ov) + _dot(lo, ov)

    jb = lax.broadcasted_iota(jnp.int32, (1, LANES), 1)
    cur = row1 >> int(math.log2(NSA_SEL_BLOCK))
    valid = (jb * NSA_SEL_BLOCK <= row1) & (jb < n_sb)
    forced = (jb == 0) | (jb == cur) | (jb == cur - 1)
    score = jnp.where(valid, imp + jnp.where(forced, NSA_FORCE_BONUS, 0.0), NEG_INF)
    sel = jnp.where(score > 0.5 * NEG_INF, _topk_mask(score, n_sel), 0.0).astype(BF16)
    for jj in range(selb_ref.shape[0]):
        selb_ref[jj] = (_dot(sel, ex_ref[:, jj * tk:(jj + 1) * tk]) - 1.0) * (-NEG_INF)

    wlen = NSA_WINDOW + t
    w0 = pl.multiple_of(jnp.maximum(t0 - NSA_WINDOW, 0), t)
    wcol = w0 + lax.broadcasted_iota(jnp.int32, (1, wlen), 1)
    dist = row4 - wcol
    wmask = (dist >= 0) & (dist < NSA_WINDOW)
    s = _dot_nt(q4, kw_ref[pl.ds(w0, wlen), :]) + slope4 * (wcol - t0).astype(F32)
    p = _softmax_exact(jnp.where(wmask, s, NEG_INF), wmask)
    o_win = _dot(p.astype(BF16), vw_ref[pl.ds(w0, wlen), :])

    def sel_step(j, c, causal):
        st = pl.multiple_of(j * tk, tk)
        col = st + lax.broadcasted_iota(jnp.int32, (1, tk), 1)
        s = (_dot_nt(q4, ks_ref[pl.ds(st, tk), :]) + slope4 * (col - t0).astype(F32)
             + stack([selb_ref[j]] * nh))
        if causal:
            s = jnp.where(col <= row4, s, NEG_INF)
        return _softmax_step(s, vs_ref[pl.ds(st, tk), :], *c)

    n_full = t0 // tk
    c = lax.fori_loop(0, n_full, lambda j, c: sel_step(j, c, False), _softmax_init(nh * t, NSA_D))
    _, l, acc = sel_step(n_full, c, True)
    o_sel = _softmax_out(l, acc)

    for h in range(nh):
        rows = slice(h * t, (h + 1) * t)
        o = (gates[:, 3 * h:3 * h + 1] * o_cmp[rows] + gates[:, 3 * h + 1:3 * h + 2] * o_sel[rows]
             + gates[:, 3 * h + 2:3 * h + 3] * o_win[rows])
        o_ref[:, h * LANES:(h + 1) * LANES] = _rms_rows(o, on_ref[...]).astype(o_ref.dtype)


def _nsa_attn(q, gates, kc, vc, kv, ov, ex, on, t=128, tk=512):
    b, s, _ = q.shape
    ncp = kc.shape[1]
    n_sb = s // NSA_SEL_BLOCK
    n_sel = min(NSA_N_SELECT, n_sb)
    cmp_spec = pl.BlockSpec((None, ncp, NSA_D), lambda bi, i: (bi, 0, 0))
    kv_spec = lambda n: pl.BlockSpec((None, s, NSA_D), lambda bi, i: (bi, 0, n))
    full = lambda a: pl.BlockSpec(a.shape, lambda bi, i: (0,) * a.ndim)
    return pl.pallas_call(
        functools.partial(_nsa_attn_kernel, t=t, tk=tk, n_sb=n_sb, n_sel=n_sel),
        grid=(b, s // t),
        in_specs=[pl.BlockSpec((None, t, NSA_HEADS * NSA_D), lambda bi, i: (bi, i, 0)),
                  pl.BlockSpec((None, t, LANES), lambda bi, i: (bi, i, 0)),
                  cmp_spec, cmp_spec, kv_spec(0), kv_spec(1), kv_spec(2), kv_spec(3),
                  full(ov), full(ex), full(on)],
        out_specs=pl.BlockSpec((None, t, NSA_HEADS * NSA_D), lambda bi, i: (bi, i, 0)),
        out_shape=jax.ShapeDtypeStruct((b, s, NSA_HEADS * NSA_D), BF16),
        scratch_shapes=[pltpu.VMEM((s // tk, t, tk), F32)],
        compiler_params=_params("parallel", "arbitrary"),
        name="nsa_attn",
    )(q, gates, kc, vc, kv, kv, kv, kv, ov, ex, on)


def _pad_cols(w, width):
    return jnp.pad(w, ((0, 0), (0, width - w.shape[1])))


def _swap_halves(w):
    half = w.shape[-1] // 2
    return jnp.concatenate([w[..., half:], w[..., :half]], axis=-1)


def _row(v):
    return v.reshape(1, -1).astype(F32)


def _nsa_constants(s, t):
    ncp = s // NSA_CMP_STRIDE
    n_sb = s // NSA_SEL_BLOCK
    c_start = NSA_CMP_STRIDE * np.arange(ncp)[:, None]
    sel_start = NSA_SEL_BLOCK * np.arange(LANES)[None, :]
    overlap = ((c_start < sel_start + NSA_SEL_BLOCK) & (c_start + NSA_CMP_LEN > sel_start)
               & (np.arange(LANES)[None, :] < n_sb))
    expand = (np.arange(s)[None, :] // NSA_SEL_BLOCK) == np.arange(LANES)[:, None]
    return jnp.asarray(overlap, BF16), jnp.asarray(expand, BF16)


def kernel(x, ffn1_norm, ffn1_w_gate, ffn1_w_up, ffn1_w_down, mix_norm, w_in, mla_cq_norm, mla_ckv_norm, mla_w_uq, mla_w_ukv, mla_qn_norm, mla_qr_norm, mla_kn_norm, mla_kr_norm, mla_o_norm, diff_q_norm, diff_k_norm, diff_lq1, diff_lk1, diff_lq2, diff_lk2, diff_subln, nsa_q_norm, nsa_pe_k, nsa_w_ck, nsa_pe_v, nsa_w_cv, nsa_kc_norm, nsa_ks_norm, nsa_kw_norm, nsa_o_norm, sb_o_norm, w_out, ffn2_norm, ffn2_w_gate, ffn2_w_up, ffn2_w_down):
    b, s, d = x.shape
    m = b * s
    depth = w_in.shape[0]

    pos = jnp.arange(s, dtype=F32)
    inv_freq = ROPE_THETA ** (-jnp.arange(0, MLA_ROPE, 2, dtype=F32) / MLA_ROPE)
    ang = pos[:, None] * inv_freq[None, :]
    cos, sin = jnp.cos(ang), jnp.sin(ang)
    cos2 = _pad_cols(jnp.concatenate([cos, cos], axis=-1), LANES)
    sin2 = _pad_cols(jnp.concatenate([-sin, sin], axis=-1), LANES)
    sb_t = 256
    tri = jnp.asarray(np.arange(sb_t)[:, None] > np.arange(sb_t)[None, :], BF16)
    overlap, expand = _nsa_constants(s, 128)

    h = x.reshape(m, d)
    for l in range(depth):
        lambda_init = 0.8 - 0.6 * math.exp(-0.3 * l)
        h = _ffn(h, ffn1_norm[l], ffn1_w_gate[l].astype(BF16), ffn1_w_up[l].astype(BF16),
                 ffn1_w_down[l].astype(BF16))

        g_mix = _row(mix_norm[l])
        wl = w_in[l]
        o0 = 0
        w_mla, o0 = wl[:, o0:o0 + W_MLA], o0 + W_MLA
        w_diff, o0 = wl[:, o0:o0 + W_DIFF], o0 + W_DIFF
        w_nsa, o0 = wl[:, o0:o0 + W_NSA], o0 + W_NSA
        w_sb = wl[:, o0:o0 + W_SB]

        w_kr = w_mla[:, MLA_Q_RANK + MLA_KV_RANK:]
        w_mla_p = jnp.concatenate([w_mla[:, :MLA_Q_RANK + MLA_KV_RANK], _pad_cols(w_kr, LANES),
                                   _pad_cols(_swap_halves(w_kr), LANES)], axis=1).astype(BF16)
        wuq = mla_w_uq[l].reshape(MLA_Q_RANK, MLA_HEADS, MLA_NOPE + MLA_ROPE)
        wuq_r = wuq[..., MLA_NOPE:]
        pad3 = lambda a: jnp.pad(a, ((0, 0), (0, 0), (0, LANES - a.shape[-1])))
        wuq_p = jnp.concatenate([wuq[..., :MLA_NOPE], pad3(wuq_r), pad3(_swap_halves(wuq_r))],
                                axis=-1).reshape(MLA_Q_RANK, MLA_HEADS * 3 * LANES).astype(BF16)
        padr = lambda v: _pad_cols(_row(v), LANES)
        q_mla, k_mla, v_mla = _mla_prep(
            h.reshape(b, s, d), g_mix, w_mla_p, _row(mla_cq_norm[l]), _row(mla_ckv_norm[l]), wuq_p,
            mla_w_ukv[l].astype(BF16), _row(mla_qn_norm[l]), padr(mla_qr_norm[l]),
            padr(_swap_halves(mla_qr_norm[l])), _row(mla_kn_norm[l]), padr(mla_kr_norm[l]),
            padr(_swap_halves(mla_kr_norm[l])), cos2, sin2)
        o_a = _mla_attn(q_mla, k_mla, v_mla, _row(mla_o_norm[l]))

        two = lambda v: jnp.concatenate([_row(v), _row(v)], axis=1)
        q1, q2, k_d, v_d = _diff_prep(h, g_mix, w_diff.astype(BF16), two(diff_q_norm[l]), two(diff_k_norm[l]))
        r3 = lambda a: a.reshape(b, s, -1)
        o_b = _diff_attn(r3(q1), r3(q2), r3(k_d), r3(v_d), _row(diff_lq1[l]), _row(diff_lk1[l]),
                         _row(diff_lq2[l]), _row(diff_lk2[l]), _row(diff_subln[l]), lambda_init)

        w_nsa_p = _pad_cols(w_nsa, NSA_HEADS * LANES + 7 * LANES).astype(BF16)
        q_n, kv_n, raw_n, gates = _nsa_prep(h, g_mix, w_nsa_p, _row(nsa_q_norm[l]), _row(nsa_ks_norm[l]),
                                            _row(nsa_kw_norm[l]))
        chunks = lambda z: z.reshape(b, s // NSA_CMP_STRIDE, NSA_CMP_STRIDE * NSA_D)
        kc, vc = _nsa_compress(chunks(raw_n[:, :LANES]), chunks(raw_n[:, LANES:]),
                               nsa_pe_k[l].reshape(1, -1), nsa_pe_v[l].reshape(1, -1),
                               nsa_w_ck[l].astype(BF16), nsa_w_cv[l].astype(BF16), _row(nsa_kc_norm[l]))
        o_c = _nsa_attn(r3(q_n), r3(gates), kc, vc, r3(kv_n), overlap, expand, _row(nsa_o_norm[l]))

        u_sb = _sb_prep(h, mix_norm[l], w_sb.astype(BF16))
        o_d = _sb_attn(r3(u_sb), tri, _row(sb_o_norm[l]), t=sb_t)

        flat = lambda a: a.reshape(m, -1)
        h = _out_proj([flat(o_a), flat(o_b), flat(o_c), flat(o_d)], w_out[l].astype(BF16), h)
        h = _ffn(h, ffn2_norm[l], ffn2_w_gate[l].astype(BF16), ffn2_w_up[l].astype(BF16),
                 ffn2_w_down[l].astype(BF16))
    return h.reshape(b, s, d)
```

```python
import functools
import math

import numpy as np
import jax
import jax.numpy as jnp
from jax import lax
from jax.experimental import pallas as pl
from jax.experimental.pallas import tpu as pltpu

F32 = jnp.float32
BF16 = jnp.bfloat16

RMS_EPS = 1e-6
NEG_INF = -1e30
LANES = 128
VMEM_LIMIT = 56 * 1024 * 1024

MLA_HEADS, MLA_Q_RANK, MLA_KV_RANK, MLA_NOPE, MLA_ROPE, MLA_V = 4, 512, 256, 128, 64, 128
ROPE_THETA = 10000.0
DIFF_HEADS, DIFF_QK, DIFF_V = 4, 64, 128
NSA_HEADS, NSA_D = 4, 128
NSA_CMP_LEN, NSA_CMP_STRIDE, NSA_SEL_BLOCK, NSA_N_SELECT, NSA_WINDOW = 32, 16, 64, 16, 512
NSA_FORCE_BONUS = 1e4
SB_HEADS, SB_D = 4, 128
N_ALIBI_HEADS = DIFF_HEADS + NSA_HEADS
ALIBI_SLOPES = [2.0 ** (-8.0 * k / N_ALIBI_HEADS) for k in range(1, N_ALIBI_HEADS + 1)]
DIFF_SLOPES, NSA_SLOPES = ALIBI_SLOPES[0::2], ALIBI_SLOPES[1::2]

W_MLA = MLA_Q_RANK + MLA_KV_RANK + MLA_ROPE
W_DIFF = DIFF_HEADS * (2 * DIFF_QK + 2 * DIFF_QK + DIFF_V)
W_NSA = NSA_HEADS * NSA_D + 6 * NSA_D + NSA_HEADS * 3
W_SB = 3 * SB_HEADS * SB_D

MLA_TQ, MLA_TK = 512, 512
DIFF_TQ, DIFF_TK = 512, 512
SB_TQ, SB_TK = 512, 512
NSA_TQ, NSA_TK = 128, 512


def _dot(a, b):
    return jnp.dot(a, b, preferred_element_type=F32)


def _dot_nt(a, b):
    return lax.dot_general(a, b, (((1,), (1,)), ((), ())), preferred_element_type=F32)


def _rms_rows(x, g):
    ms = jnp.mean(x * x, axis=-1, keepdims=True)
    return x * lax.rsqrt(ms + RMS_EPS) * g


def _rms_cols(x, g):
    ms = jnp.mean(x * x, axis=0, keepdims=True)
    return x * lax.rsqrt(ms + RMS_EPS) * g


def _params(*sem):
    return pltpu.CompilerParams(dimension_semantics=sem, vmem_limit_bytes=VMEM_LIMIT)


def _full(a, grid_rank):
    return pl.BlockSpec(a.shape, lambda *_: (0,) * a.ndim)


def _ffn_kernel(x_ref, g_ref, wg_ref, wu_ref, wd_ref, o_ref, xn_ref, acc_ref, *, nf):
    f = pl.program_id(1)

    @pl.when(f == 0)
    def _():
        xn_ref[...] = _rms_rows(x_ref[...], g_ref[...]).astype(BF16)
        acc_ref[...] = jnp.zeros_like(acc_ref)

    xn = xn_ref[...]
    gate = _dot(xn, wg_ref[...])
    up = _dot(xn, wu_ref[...])
    mid = (gate * jax.nn.sigmoid(gate) * up).astype(BF16)
    acc_ref[...] += _dot(mid, wd_ref[...])

    @pl.when(f == nf - 1)
    def _():
        o_ref[...] = x_ref[...] + 0.5 * acc_ref[...]


def _ffn(h, g, wg, wu, wd, tm=512, tf=512):
    m, d = h.shape
    f = wg.shape[1]
    return pl.pallas_call(
        functools.partial(_ffn_kernel, nf=f // tf),
        grid=(m // tm, f // tf),
        in_specs=[
            pl.BlockSpec((tm, d), lambda i, j: (i, 0)),
            pl.BlockSpec((1, d), lambda i, j: (0, 0)),
            pl.BlockSpec((d, tf), lambda i, j: (0, j)),
            pl.BlockSpec((d, tf), lambda i, j: (0, j)),
            pl.BlockSpec((tf, d), lambda i, j: (j, 0)),
        ],
        out_specs=pl.BlockSpec((tm, d), lambda i, j: (i, 0)),
        out_shape=jax.ShapeDtypeStruct((m, d), F32),
        scratch_shapes=[pltpu.VMEM((tm, d), BF16), pltpu.VMEM((tm, d), F32)],
        compiler_params=_params("parallel", "arbitrary"),
        name="ffn",
    )(h, g.reshape(1, d), wg, wu, wd)


def _out_proj_kernel(a0_ref, a1_ref, a2_ref, a3_ref, w_ref, r_ref, o_ref):
    acc = r_ref[...]
    for n, a_ref in enumerate((a0_ref, a1_ref, a2_ref, a3_ref)):
        kw = a_ref.shape[1]
        acc = acc + _dot(a_ref[...], w_ref[n * kw:(n + 1) * kw, :])
    o_ref[...] = acc


def _out_proj(parts, w, res, tm=512, tn=1024):
    m, d = res.shape
    kw = parts[0].shape[1]
    a_spec = pl.BlockSpec((tm, kw), lambda i, j: (i, 0))
    return pl.pallas_call(
        _out_proj_kernel,
        grid=(m // tm, d // tn),
        in_specs=[a_spec, a_spec, a_spec, a_spec,
                  pl.BlockSpec((w.shape[0], tn), lambda i, j: (0, j)),
                  pl.BlockSpec((tm, tn), lambda i, j: (i, j))],
        out_specs=pl.BlockSpec((tm, tn), lambda i, j: (i, j)),
        out_shape=jax.ShapeDtypeStruct((m, d), F32),
        compiler_params=_params("parallel", "arbitrary"),
        name="out_proj",
    )(*parts, w, res)


def _sb_prep_kernel(x_ref, g_ref, w_ref, qk_out, vt_out):
    xn = _rms_rows(x_ref[...], g_ref[...]).astype(BF16)
    u = _dot(xn, w_ref[...])
    width = SB_HEADS * SB_D
    qk_out[:, :width] = (u[:, :width] * (SB_D ** -0.5)).astype(BF16)
    qk_out[:, width:] = u[:, width:2 * width].astype(BF16)
    vt_out[...] = u[:, 2 * width:].T.astype(BF16)


def _sb_prep(h, g, w, tm):
    m, d = h.shape
    width = SB_HEADS * SB_D
    return pl.pallas_call(
        _sb_prep_kernel,
        grid=(m // tm,),
        in_specs=[pl.BlockSpec((tm, d), lambda i: (i, 0)), _full(g, 1), _full(w, 1)],
        out_specs=[pl.BlockSpec((tm, 2 * width), lambda i: (i, 0)),
                   pl.BlockSpec((None, width, tm), lambda i: (i, 0, 0))],
        out_shape=[jax.ShapeDtypeStruct((m, 2 * width), BF16),
                   jax.ShapeDtypeStruct((m // tm, width, tm), BF16)],
        compiler_params=_params("parallel"),
        name="sb_prep",
    )(h, g, w)


def _rope_rows(x, x_sw, g, g_sw, cos2, sin2):
    ms = jnp.sum(x * x, axis=-1, keepdims=True) * (1.0 / MLA_ROPE)
    r = lax.rsqrt(ms + RMS_EPS)
    return (x * r * g) * cos2 + (x_sw * r * g_sw) * sin2


def _mla_prep_kernel(x_ref, g_ref, w_ref, cqn_ref, ckvn_ref, wuq_ref, wukv_ref, qn_ref, qr_ref,
                     qrs_ref, kn_ref, kr_ref, krs_ref, cos_ref, sin_ref, q_out, k_out, vt_out):
    xn = _rms_rows(x_ref[...], g_ref[...]).astype(BF16)
    u = _dot(xn, w_ref[...])
    c_q = u[:, :MLA_Q_RANK]
    c_kv = u[:, MLA_Q_RANK:MLA_Q_RANK + MLA_KV_RANK]
    kr = u[:, MLA_Q_RANK + MLA_KV_RANK:MLA_Q_RANK + MLA_KV_RANK + LANES]
    kr_sw = u[:, MLA_Q_RANK + MLA_KV_RANK + LANES:]
    cos2, sin2 = cos_ref[...], sin_ref[...]
    qall = _dot(_rms_rows(c_q, cqn_ref[...]).astype(BF16), wuq_ref[...])
    kvall = _dot(_rms_rows(c_kv, ckvn_ref[...]).astype(BF16), wukv_ref[...])
    k_rot = _rope_rows(kr, kr_sw, kr_ref[...], krs_ref[...], cos2, sin2).astype(BF16)
    scale = (MLA_NOPE + MLA_ROPE) ** -0.5
    for h in range(MLA_HEADS):
        qh = qall[:, h * 3 * LANES:(h + 1) * 3 * LANES]
        q_nope = _rms_rows(qh[:, :LANES], qn_ref[...]) * scale
        q_rot = _rope_rows(qh[:, LANES:2 * LANES], qh[:, 2 * LANES:], qr_ref[...], qrs_ref[...],
                           cos2, sin2) * scale
        q_out[:, h * 2 * LANES:h * 2 * LANES + LANES] = q_nope.astype(BF16)
        q_out[:, h * 2 * LANES + LANES:(h + 1) * 2 * LANES] = q_rot.astype(BF16)
        kvh = kvall[:, h * 2 * LANES:(h + 1) * 2 * LANES]
        k_out[:, h * 2 * LANES:h * 2 * LANES + LANES] = _rms_rows(kvh[:, :LANES], kn_ref[...]).astype(BF16)
        k_out[:, h * 2 * LANES + LANES:(h + 1) * 2 * LANES] = k_rot
        vt_out[h * LANES:(h + 1) * LANES, :] = kvh[:, LANES:].T.astype(BF16)


def _mla_prep(h3, g, w, cqn, ckvn, wuq, wukv, qn, qr, qrs, kn, kr, krs, cos2, sin2, tm):
    b, s, d = h3.shape
    smalls = [g, w, cqn, ckvn, wuq, wukv, qn, qr, qrs, kn, kr, krs]
    tab = pl.BlockSpec((tm, LANES), lambda bi, i: (i, 0))
    out = lambda c: pl.BlockSpec((None, tm, c), lambda bi, i: (bi, i, 0))
    return pl.pallas_call(
        _mla_prep_kernel,
        grid=(b, s // tm),
        in_specs=[pl.BlockSpec((None, tm, d), lambda bi, i: (bi, i, 0))] + [_full(a, 2) for a in smalls] + [tab, tab],
        out_specs=[out(MLA_HEADS * 2 * LANES), out(MLA_HEADS * 2 * LANES),
                   pl.BlockSpec((None, None, MLA_HEADS * MLA_V, tm), lambda bi, i: (bi, i, 0, 0))],
        out_shape=[jax.ShapeDtypeStruct((b, s, MLA_HEADS * 2 * LANES), BF16),
                   jax.ShapeDtypeStruct((b, s, MLA_HEADS * 2 * LANES), BF16),
                   jax.ShapeDtypeStruct((b, s // tm, MLA_HEADS * MLA_V, tm), BF16)],
        compiler_params=_params("parallel", "parallel"),
        name="mla_prep",
    )(h3, *smalls, cos2, sin2)


def _diff_prep_kernel(x_ref, g_ref, w_ref, qn_ref, kn_ref, q1_out, q2_out, k_out, vt_out):
    xn = _rms_rows(x_ref[...], g_ref[...]).astype(BF16)
    u = _dot(xn, w_ref[...])
    width = DIFF_HEADS * LANES
    lane = lax.broadcasted_iota(jnp.int32, (1, LANES), 1)
    lo = lane < DIFF_QK
    scale = DIFF_QK ** -0.5

    def half_rms(x, g2):
        sq = x * x
        s_lo = jnp.sum(jnp.where(lo, sq, 0.0), axis=-1, keepdims=True)
        s_hi = jnp.sum(jnp.where(lo, 0.0, sq), axis=-1, keepdims=True)
        r = jnp.where(lo, lax.rsqrt(s_lo * (1.0 / DIFF_QK) + RMS_EPS),
                      lax.rsqrt(s_hi * (1.0 / DIFF_QK) + RMS_EPS))
        return x * r * g2

    for h in range(DIFF_HEADS):
        sl = slice(h * LANES, (h + 1) * LANES)
        qn = half_rms(u[:, sl], qn_ref[...]) * scale
        q1_out[:, sl] = jnp.where(lo, qn, 0.0).astype(BF16)
        q2_out[:, sl] = jnp.where(lo, 0.0, qn).astype(BF16)
        k_out[:, sl] = half_rms(u[:, width + h * LANES:width + (h + 1) * LANES], kn_ref[...]).astype(BF16)
    vt_out[...] = u[:, 2 * width:].T.astype(BF16)


def _diff_prep(h, g, w, qn2, kn2, tm):
    m, d = h.shape
    width = DIFF_HEADS * LANES
    out = pl.BlockSpec((tm, width), lambda i: (i, 0))
    return pl.pallas_call(
        _diff_prep_kernel,
        grid=(m // tm,),
        in_specs=[pl.BlockSpec((tm, d), lambda i: (i, 0)), _full(g, 1), _full(w, 1), _full(qn2, 1), _full(kn2, 1)],
        out_specs=[out, out, out, pl.BlockSpec((None, width, tm), lambda i: (i, 0, 0))],
        out_shape=[jax.ShapeDtypeStruct((m, width), BF16)] * 3
        + [jax.ShapeDtypeStruct((m // tm, width, tm), BF16)],
        compiler_params=_params("parallel"),
        name="diff_prep",
    )(h, g, w, qn2, kn2)


def _nsa_prep_kernel(x_ref, g_ref, w_ref, qn_ref, ksn_ref, kwn_ref, q_out, k_out, raw_out, vst_out,
                     vwt_out, gt_out):
    xn = _rms_rows(x_ref[...], g_ref[...]).astype(BF16)
    u = _dot(xn, w_ref[...])
    scale = NSA_D ** -0.5
    for h in range(NSA_HEADS):
        sl = slice(h * LANES, (h + 1) * LANES)
        q_out[:, sl] = (_rms_rows(u[:, sl], qn_ref[...]) * scale).astype(BF16)
    base = NSA_HEADS * LANES
    piece = lambda n: u[:, base + n * LANES:base + (n + 1) * LANES]
    raw_out[:, :LANES] = piece(0)
    raw_out[:, LANES:] = piece(1)
    k_out[:, :LANES] = _rms_rows(piece(2), ksn_ref[...]).astype(BF16)
    k_out[:, LANES:] = _rms_rows(piece(4), kwn_ref[...]).astype(BF16)
    vst_out[...] = piece(3).T.astype(BF16)
    vw_t = piece(5).T.astype(BF16)
    gate_t = jax.nn.sigmoid(piece(6)).T
    for c in range(vwt_out.shape[0]):
        vwt_out[c] = vw_t[:, c * LANES:(c + 1) * LANES]
        gt_out[c] = gate_t[:, c * LANES:(c + 1) * LANES]


def _nsa_prep(h, g, w, qn, ksn, kwn, tm):
    m, d = h.shape
    nblk = tm // LANES
    out = lambda c: pl.BlockSpec((tm, c), lambda i: (i, 0))
    blocks = pl.BlockSpec((nblk, LANES, LANES), lambda i: (i, 0, 0))
    return pl.pallas_call(
        _nsa_prep_kernel,
        grid=(m // tm,),
        in_specs=[pl.BlockSpec((tm, d), lambda i: (i, 0))] + [_full(a, 1) for a in (g, w, qn, ksn, kwn)],
        out_specs=[out(NSA_HEADS * LANES), out(2 * LANES), out(2 * LANES),
                   pl.BlockSpec((None, NSA_D, tm), lambda i: (i, 0, 0)), blocks, blocks],
        out_shape=[jax.ShapeDtypeStruct((m, NSA_HEADS * LANES), BF16),
                   jax.ShapeDtypeStruct((m, 2 * LANES), BF16),
                   jax.ShapeDtypeStruct((m, 2 * LANES), F32),
                   jax.ShapeDtypeStruct((m // tm, NSA_D, tm), BF16),
                   jax.ShapeDtypeStruct((m // LANES, LANES, LANES), BF16),
                   jax.ShapeDtypeStruct((m // LANES, LANES, LANES), F32)],
        compiler_params=_params("parallel"),
        name="nsa_prep",
    )(h, g, w, qn, ksn, kwn)


def _nsa_compress_kernel(zk_ref, zv_ref, pek_ref, pev_ref, wk_ref, wv_ref, kcn_ref, kc_out, vct_out):
    half = NSA_CMP_STRIDE * NSA_D
    nchunk = zk_ref.shape[0]

    def compress(z, pe_ref, w_ref):
        first = _dot((z + pe_ref[:, :half]).astype(BF16), w_ref[:half, :])
        second = _dot((z + pe_ref[:, half:]).astype(BF16), w_ref[half:, :])
        return first + pltpu.roll(second, shift=nchunk - 1, axis=0)

    kc_out[...] = _rms_rows(compress(zk_ref[...], pek_ref, wk_ref), kcn_ref[...]).astype(BF16)
    vct_out[...] = compress(zv_ref[...], pev_ref, wv_ref).T.astype(BF16)


def _nsa_compress(zk, zv, pek, pev, wk, wv, kcn):
    b, nchunk, width = zk.shape
    z_spec = pl.BlockSpec((None, nchunk, width), lambda bi: (bi, 0, 0))
    return pl.pallas_call(
        _nsa_compress_kernel,
        grid=(b,),
        in_specs=[z_spec, z_spec] + [_full(a, 1) for a in (pek, pev, wk, wv, kcn)],
        out_specs=[pl.BlockSpec((None, nchunk, NSA_D), lambda bi: (bi, 0, 0)),
                   pl.BlockSpec((None, NSA_D, nchunk), lambda bi: (bi, 0, 0))],
        out_shape=[jax.ShapeDtypeStruct((b, nchunk, NSA_D), BF16),
                   jax.ShapeDtypeStruct((b, NSA_D, nchunk), BF16)],
        compiler_params=_params("parallel"),
        name="nsa_compress",
    )(zk, zv, pek, pev, wk, wv, kcn)


def _softmax_step(s, vt, m, l, acc):
    m_new = jnp.maximum(m, jnp.max(s, axis=0, keepdims=True))
    alpha = jnp.exp(m - m_new)
    p = jnp.exp(s - m_new)
    l = alpha * l + jnp.sum(p, axis=0, keepdims=True)
    acc = alpha * acc + _dot(vt, p.astype(BF16))
    return m_new, l, acc


def _softmax_init(nq, dv):
    return (jnp.full((1, nq), NEG_INF, F32), jnp.zeros((1, nq), F32), jnp.zeros((dv, nq), F32))


def _softmax_out(l, acc):
    return acc * (1.0 / jnp.maximum(l, 1e-30))


def _softmax_exact(s, mask):
    m = jnp.max(s, axis=0, keepdims=True)
    e = jnp.where(mask, jnp.exp(s - m), 0.0)
    return e * (1.0 / jnp.maximum(jnp.sum(e, axis=0, keepdims=True), 1e-30))


def _causal_mask(tk, tq, d, strict=False, reps=1):
    key = d * tk + lax.broadcasted_iota(jnp.int32, (tk, reps * tq), 0)
    qry = lax.broadcasted_iota(jnp.int32, (tk, reps * tq), 1)
    for _ in range(1, reps):
        qry = jnp.where(qry >= tq, qry - tq, qry)
    return key < qry if strict else key <= qry


def _lanes(parts):
    return jnp.concatenate(parts, axis=1)


def _mla_attn_kernel(q_ref, k_ref, vt_ref, on_ref, o_ref, *, tq, tk):
    i = pl.program_id(1)
    r = tq // tk
    dq = 2 * LANES
    qs = [q_ref[:, h * dq:(h + 1) * dq] for h in range(MLA_HEADS)]

    def step(j, carry, mask):
        st = pl.multiple_of(j * tk, tk)
        out = ()
        for h in range(MLA_HEADS):
            s = _dot_nt(k_ref[pl.ds(st, tk), h * dq:(h + 1) * dq], qs[h])
            if mask is not None:
                s = jnp.where(mask, s, NEG_INF)
            out += _softmax_step(s, vt_ref[j, h * MLA_V:(h + 1) * MLA_V, :], *carry[3 * h:3 * h + 3])
        return out

    c = lax.fori_loop(0, i * r, lambda j, c: step(j, c, None), _softmax_init(tq, MLA_V) * MLA_HEADS)
    for d in range(r):
        c = step(i * r + d, c, _causal_mask(tk, tq, d))
    for h in range(MLA_HEADS):
        o = _softmax_out(c[3 * h + 1], c[3 * h + 2])
        o_ref[:, h * MLA_V:(h + 1) * MLA_V] = _rms_cols(o, on_ref[...]).T.astype(o_ref.dtype)


def _mla_attn(q, k, vt, on, tq, tk):
    b, s, _ = q.shape
    return pl.pallas_call(
        functools.partial(_mla_attn_kernel, tq=tq, tk=tk),
        grid=(b, s // tq),
        in_specs=[pl.BlockSpec((None, tq, q.shape[2]), lambda bi, i: (bi, i, 0)),
                  pl.BlockSpec((None, s, k.shape[2]), lambda bi, i: (bi, 0, 0)),
                  pl.BlockSpec((None,) + vt.shape[1:], lambda bi, i: (bi, 0, 0, 0)),
                  _full(on, 2)],
        out_specs=pl.BlockSpec((None, tq, MLA_HEADS * MLA_V), lambda bi, i: (bi, i, 0)),
        out_shape=jax.ShapeDtypeStruct((b, s, MLA_HEADS * MLA_V), BF16),
        compiler_params=_params("parallel", "arbitrary"),
        name="mla_attn",
    )(q, k, vt, on)


def _diff_attn_kernel(q1_ref, q2_ref, k_ref, vt_ref, lq1_ref, lk1_ref, lq2_ref, lk2_ref,
                      sub_ref, o_ref, *, tq, tk, lambda_init):
    i = pl.program_id(1)
    r = tq // tk
    t0 = i * tq
    keyrel = lax.broadcasted_iota(jnp.int32, (tk, 1), 0)
    qs = [jnp.concatenate([q1_ref[:, h * LANES:(h + 1) * LANES], q2_ref[:, h * LANES:(h + 1) * LANES]], axis=0)
          for h in range(DIFF_HEADS)]

    def step(j, carry, mask):
        st = pl.multiple_of(j * tk, tk)
        rel = (keyrel + (j * tk - t0)).astype(F32)
        out = ()
        for h in range(DIFF_HEADS):
            sl = slice(h * LANES, (h + 1) * LANES)
            s = _dot_nt(k_ref[pl.ds(st, tk), sl], qs[h]) + DIFF_SLOPES[h] * rel
            if mask is not None:
                s = jnp.where(mask, s, NEG_INF)
            out += _softmax_step(s, vt_ref[j, sl, :], *carry[3 * h:3 * h + 3])
        return out

    c = lax.fori_loop(0, i * r, lambda j, c: step(j, c, None), _softmax_init(2 * tq, DIFF_V) * DIFF_HEADS)
    for d in range(r):
        c = step(i * r + d, c, _causal_mask(tk, tq, d, reps=2))
    lam = (jnp.exp(jnp.sum(lq1_ref[...] * lk1_ref[...], axis=-1, keepdims=True))
           - jnp.exp(jnp.sum(lq2_ref[...] * lk2_ref[...], axis=-1, keepdims=True)) + lambda_init)
    for h in range(DIFF_HEADS):
        o12 = _softmax_out(c[3 * h + 1], c[3 * h + 2])
        o = o12[:, :tq] - lam * o12[:, tq:]
        o_ref[:, h * DIFF_V:(h + 1) * DIFF_V] = (
            _rms_cols(o, sub_ref[...]) * (1.0 - lambda_init)).T.astype(o_ref.dtype)


def _diff_attn(q1, q2, k, vt, lq1, lk1, lq2, lk2, sub, lambda_init, tq, tk):
    b, s, w = q1.shape
    qs = pl.BlockSpec((None, tq, w), lambda bi, i: (bi, i, 0))
    smalls = [lq1, lk1, lq2, lk2, sub]
    return pl.pallas_call(
        functools.partial(_diff_attn_kernel, tq=tq, tk=tk, lambda_init=lambda_init),
        grid=(b, s // tq),
        in_specs=[qs, qs, pl.BlockSpec((None, s, w), lambda bi, i: (bi, 0, 0)),
                  pl.BlockSpec((None,) + vt.shape[1:], lambda bi, i: (bi, 0, 0, 0))]
        + [_full(a, 2) for a in smalls],
        out_specs=pl.BlockSpec((None, tq, DIFF_HEADS * DIFF_V), lambda bi, i: (bi, i, 0)),
        out_shape=jax.ShapeDtypeStruct((b, s, DIFF_HEADS * DIFF_V), BF16),
        compiler_params=_params("parallel", "arbitrary"),
        name="diff_attn",
    )(q1, q2, k, vt, *smalls)


def _sb_attn_kernel(q_ref, k_ref, vt_ref, tri_ref, on_ref, o_ref, *, tq, tk):
    i = pl.program_id(1)
    r = tq // tk
    tri = tri_ref[...]
    qs = [q_ref[:, h * SB_D:(h + 1) * SB_D] for h in range(SB_HEADS)]

    def step(j, carry, mask):
        st = pl.multiple_of(j * tk, tk)
        out = ()
        for h in range(SB_HEADS):
            run, acc = carry[2 * h:2 * h + 2]
            sl = slice(h * SB_D, (h + 1) * SB_D)
            z = _dot_nt(k_ref[pl.ds(st, tk), sl], qs[h])
            log_sig = jnp.minimum(z, 0.0) - jnp.log(1.0 + jnp.exp(-jnp.abs(z)))
            log_not = log_sig - z
            if mask is not None:
                log_not = jnp.where(mask, log_not, 0.0)
            hi = log_not.astype(BF16)
            lo = (log_not - hi.astype(F32)).astype(BF16)
            after = _dot(tri, hi) + _dot(tri, lo) + run
            a = jnp.exp(log_sig + after)
            if mask is not None:
                a = jnp.where(mask, a, 0.0)
            acc = acc + _dot(vt_ref[j, sl, :], a.astype(BF16))
            run = run + jnp.sum(log_not, axis=0, keepdims=True)
            out += (run, acc)
        return out

    carry = (jnp.zeros((1, tq), F32), jnp.zeros((SB_D, tq), F32)) * SB_HEADS
    for d in reversed(range(r)):
        carry = step(i * r + d, carry, _causal_mask(tk, tq, d, strict=True))
    carry = lax.fori_loop(0, i * r, lambda n, c: step(i * r - 1 - n, c, None), carry)
    for h in range(SB_HEADS):
        o_ref[:, h * SB_D:(h + 1) * SB_D] = _rms_cols(carry[2 * h + 1], on_ref[...]).T.astype(o_ref.dtype)


def _sb_attn(qk, vt, tri, on, tq, tk):
    b, s, _ = qk.shape
    w = SB_HEADS * SB_D
    return pl.pallas_call(
        functools.partial(_sb_attn_kernel, tq=tq, tk=tk),
        grid=(b, s // tq),
        in_specs=[pl.BlockSpec((None, tq, w), lambda bi, i: (bi, i, 0)),
                  pl.BlockSpec((None, s, w), lambda bi, i: (bi, 0, 1)),
                  pl.BlockSpec((None,) + vt.shape[1:], lambda bi, i: (bi, 0, 0, 0)),
                  _full(tri, 2), _full(on, 2)],
        out_specs=pl.BlockSpec((None, tq, w), lambda bi, i: (bi, i, 0)),
        out_shape=jax.ShapeDtypeStruct((b, s, w), BF16),
        compiler_params=_params("parallel", "arbitrary"),
        name="sb_attn",
    )(qk, qk, vt, tri, on)


def _topk_mask(score, n):
    blk = lax.broadcasted_iota(jnp.int32, score.shape, 0).astype(F32)

    def body(_, c):
        work, sel = c
        mx = jnp.max(work, axis=0, keepdims=True)
        idx = jnp.min(jnp.where(work == mx, blk, 1e9), axis=0, keepdims=True)
        pick = blk == idx
        return jnp.where(pick, -jnp.inf, work), jnp.where(pick, 1.0, sel)

    _, sel = lax.fori_loop(0, n, body, (score, jnp.zeros_like(score)), unroll=True)
    return sel


def _nsa_attn_kernel(q_ref, gt_ref, kc_ref, vct_ref, ks_ref, kw_ref, vst_ref, vwt_ref, ovt_ref, ext_ref,
                     on_ref, o_ref, selb_ref, *, t, tk, n_sb, n_sel):
    i = pl.program_id(1)
    t0 = i * t
    ncp = kc_ref.shape[0]
    nh = NSA_HEADS
    q4 = jnp.concatenate([q_ref[:, h * LANES:(h + 1) * LANES] for h in range(nh)], axis=0)
    qpos1 = t0 + lax.broadcasted_iota(jnp.int32, (1, t), 1)
    qpos4 = _lanes([qpos1] * nh)
    slope4 = _lanes([jnp.full((1, t), NSA_SLOPES[h], F32) for h in range(nh)])

    c_end = NSA_CMP_STRIDE * lax.broadcasted_iota(jnp.int32, (ncp, 1), 0) + (NSA_CMP_LEN - 1)
    cmask = c_end <= qpos4
    s = jnp.where(cmask, _dot_nt(kc_ref[...], q4) + slope4 * (c_end - t0).astype(F32), NEG_INF)
    p = _softmax_exact(s, cmask)
    o_cmp = _dot(vct_ref[...], p.astype(BF16))
    pc_sum = p[:, 0:t]
    for h in range(1, nh):
        pc_sum = pc_sum + p[:, h * t:(h + 1) * t]
    ovt = ovt_ref[...]
    hi = pc_sum.astype(BF16)
    r1 = pc_sum - hi.astype(F32)
    mid = r1.astype(BF16)
    lo = (r1 - mid.astype(F32)).astype(BF16)
    imp = _dot(ovt, hi) + _dot(ovt, mid) + _dot(ovt, lo)

    jb = lax.broadcasted_iota(jnp.int32, (LANES, 1), 0)
    cur = qpos1 >> int(math.log2(NSA_SEL_BLOCK))
    valid = (jb * NSA_SEL_BLOCK <= qpos1) & (jb < n_sb)
    forced = (jb == 0) | (jb == cur) | (jb == cur - 1)
    score = jnp.where(valid, imp + jnp.where(forced, NSA_FORCE_BONUS, 0.0), NEG_INF)
    sel = jnp.where(score > 0.5 * NEG_INF, _topk_mask(score, n_sel), 0.0).astype(BF16)
    for jj in range(selb_ref.shape[0]):
        selb_ref[jj] = (_dot(ext_ref[jj * tk:(jj + 1) * tk, :], sel) - 1.0) * (-NEG_INF)

    nwin = NSA_WINDOW // t + 1
    wblk = jnp.maximum(i - NSA_WINDOW // t, 0)
    w0 = pl.multiple_of(wblk * t, t)
    wcol = w0 + lax.broadcasted_iota(jnp.int32, (nwin * t, 1), 0)
    dist = qpos4 - wcol
    wmask = (dist >= 0) & (dist < NSA_WINDOW)
    s = _dot_nt(kw_ref[pl.ds(w0, nwin * t), :], q4) + slope4 * (wcol - t0).astype(F32)
    p = _softmax_exact(jnp.where(wmask, s, NEG_INF), wmask).astype(BF16)
    o_win = _dot(vwt_ref[wblk], p[0:t])
    for c in range(1, nwin):
        o_win = o_win + _dot(vwt_ref[wblk + c], p[c * t:(c + 1) * t])

    def sel_step(j, c, causal):
        st = pl.multiple_of(j * tk, tk)
        col = st + lax.broadcasted_iota(jnp.int32, (tk, 1), 0)
        s = (_dot_nt(ks_ref[pl.ds(st, tk), :], q4) + slope4 * (col - t0).astype(F32)
             + _lanes([selb_ref[j]] * nh))
        if causal:
            s = jnp.where(col <= qpos4, s, NEG_INF)
        return _softmax_step(s, vst_ref[j], *c)

    n_full = t0 // tk
    c = lax.fori_loop(0, n_full, lambda j, c: sel_step(j, c, False), _softmax_init(nh * t, NSA_D))
    _, l, acc = sel_step(n_full, c, True)
    o_sel = _softmax_out(l, acc)

    gates = gt_ref[...]
    for h in range(nh):
        cols = slice(h * t, (h + 1) * t)
        o = (gates[3 * h:3 * h + 1] * o_cmp[:, cols] + gates[3 * h + 1:3 * h + 2] * o_sel[:, cols]
             + gates[3 * h + 2:3 * h + 3] * o_win[:, cols])
        o_ref[:, h * LANES:(h + 1) * LANES] = _rms_cols(o, on_ref[...]).T.astype(o_ref.dtype)


def _nsa_attn(q, gt, kc, vct, kk, vst, vwt, ovt, ext, on, t, tk):
    b, s, _ = q.shape
    ncp = kc.shape[1]
    n_sb = s // NSA_SEL_BLOCK
    n_sel = min(NSA_N_SELECT, n_sb)
    k_spec = lambda n: pl.BlockSpec((None, s, NSA_D), lambda bi, i: (bi, 0, n))
    whole = lambda a: pl.BlockSpec((None,) + a.shape[1:], lambda bi, i: (bi,) + (0,) * (a.ndim - 1))
    return pl.pallas_call(
        functools.partial(_nsa_attn_kernel, t=t, tk=tk, n_sb=n_sb, n_sel=n_sel),
        grid=(b, s // t),
        in_specs=[pl.BlockSpec((None, t, NSA_HEADS * NSA_D), lambda bi, i: (bi, i, 0)),
                  pl.BlockSpec((None, None, LANES, t), lambda bi, i: (bi, i, 0, 0)),
                  whole(kc), whole(vct), k_spec(0), k_spec(1), whole(vst), whole(vwt),
                  _full(ovt, 2), _full(ext, 2), _full(on, 2)],
        out_specs=pl.BlockSpec((None, t, NSA_HEADS * NSA_D), lambda bi, i: (bi, i, 0)),
        out_shape=jax.ShapeDtypeStruct((b, s, NSA_HEADS * NSA_D), BF16),
        scratch_shapes=[pltpu.VMEM((s // tk, tk, t), F32)],
        compiler_params=_params("parallel", "arbitrary"),
        name="nsa_attn",
    )(q, gt, kc, vct, kk, kk, vst, vwt, ovt, ext, on)


def _pad_cols(w, width):
    return jnp.pad(w, ((0, 0), (0, width - w.shape[1])))


def _swap_halves(w):
    half = w.shape[-1] // 2
    return jnp.concatenate([w[..., half:], w[..., :half]], axis=-1)


def _row(v):
    return v.reshape(1, -1).astype(F32)


def _col(v):
    return v.reshape(-1, 1).astype(F32)


def _nsa_constants(s):
    ncp = s // NSA_CMP_STRIDE
    n_sb = s // NSA_SEL_BLOCK
    c_start = NSA_CMP_STRIDE * np.arange(ncp)[None, :]
    sel_start = NSA_SEL_BLOCK * np.arange(LANES)[:, None]
    overlap_t = ((c_start < sel_start + NSA_SEL_BLOCK) & (c_start + NSA_CMP_LEN > sel_start)
                 & (np.arange(LANES)[:, None] < n_sb))
    expand_t = (np.arange(s)[:, None] // NSA_SEL_BLOCK) == np.arange(LANES)[None, :]
    return jnp.asarray(overlap_t, BF16), jnp.asarray(expand_t, BF16)


def kernel(x, ffn1_norm, ffn1_w_gate, ffn1_w_up, ffn1_w_down, mix_norm, w_in, mla_cq_norm, mla_ckv_norm, mla_w_uq, mla_w_ukv, mla_qn_norm, mla_qr_norm, mla_kn_norm, mla_kr_norm, mla_o_norm, diff_q_norm, diff_k_norm, diff_lq1, diff_lk1, diff_lq2, diff_lk2, diff_subln, nsa_q_norm, nsa_pe_k, nsa_w_ck, nsa_pe_v, nsa_w_cv, nsa_kc_norm, nsa_ks_norm, nsa_kw_norm, nsa_o_norm, sb_o_norm, w_out, ffn2_norm, ffn2_w_gate, ffn2_w_up, ffn2_w_down):
    b, s, d = x.shape
    m = b * s
    depth = w_in.shape[0]

    pos = jnp.arange(s, dtype=F32)
    inv_freq = ROPE_THETA ** (-jnp.arange(0, MLA_ROPE, 2, dtype=F32) / MLA_ROPE)
    ang = pos[:, None] * inv_freq[None, :]
    cos, sin = jnp.cos(ang), jnp.sin(ang)
    cos2 = _pad_cols(jnp.concatenate([cos, cos], axis=-1), LANES)
    sin2 = _pad_cols(jnp.concatenate([-sin, sin], axis=-1), LANES)
    tri = jnp.asarray(np.arange(SB_TK)[None, :] > np.arange(SB_TK)[:, None], BF16)
    overlap_t, expand_t = _nsa_constants(s)

    h = x.reshape(m, d)
    for l in range(depth):
        lambda_init = 0.8 - 0.6 * math.exp(-0.3 * l)
        h = _ffn(h, ffn1_norm[l], ffn1_w_gate[l].astype(BF16), ffn1_w_up[l].astype(BF16),
                 ffn1_w_down[l].astype(BF16))

        g_mix = _row(mix_norm[l])
        wl = w_in[l]
        o0 = 0
        w_mla, o0 = wl[:, o0:o0 + W_MLA], o0 + W_MLA
        w_diff, o0 = wl[:, o0:o0 + W_DIFF], o0 + W_DIFF
        w_nsa, o0 = wl[:, o0:o0 + W_NSA], o0 + W_NSA
        w_sb = wl[:, o0:o0 + W_SB]
        per_batch = lambda a: a.reshape((b, a.shape[0] // b) + a.shape[1:])

        w_kr = w_mla[:, MLA_Q_RANK + MLA_KV_RANK:]
        w_mla_p = jnp.concatenate([w_mla[:, :MLA_Q_RANK + MLA_KV_RANK], _pad_cols(w_kr, LANES),
                                   _pad_cols(_swap_halves(w_kr), LANES)], axis=1).astype(BF16)
        wuq = mla_w_uq[l].reshape(MLA_Q_RANK, MLA_HEADS, MLA_NOPE + MLA_ROPE)
        wuq_r = wuq[..., MLA_NOPE:]
        pad3 = lambda a: jnp.pad(a, ((0, 0), (0, 0), (0, LANES - a.shape[-1])))
        wuq_p = jnp.concatenate([wuq[..., :MLA_NOPE], pad3(wuq_r), pad3(_swap_halves(wuq_r))],
                                axis=-1).reshape(MLA_Q_RANK, MLA_HEADS * 3 * LANES).astype(BF16)
        padr = lambda v: _pad_cols(_row(v), LANES)
        q_mla, k_mla, vt_mla = _mla_prep(
            h.reshape(b, s, d), g_mix, w_mla_p, _row(mla_cq_norm[l]), _row(mla_ckv_norm[l]), wuq_p,
            mla_w_ukv[l].astype(BF16), _row(mla_qn_norm[l]), padr(mla_qr_norm[l]),
            padr(_swap_halves(mla_qr_norm[l])), _row(mla_kn_norm[l]), padr(mla_kr_norm[l]),
            padr(_swap_halves(mla_kr_norm[l])), cos2, sin2, tm=MLA_TK)
        o_a = _mla_attn(q_mla, k_mla, vt_mla, _col(mla_o_norm[l]), MLA_TQ, MLA_TK)

        two = lambda v: jnp.concatenate([_row(v), _row(v)], axis=1)
        q1, q2, k_d, vt_d = _diff_prep(h, g_mix, w_diff.astype(BF16), two(diff_q_norm[l]),
                                       two(diff_k_norm[l]), tm=DIFF_TK)
        o_b = _diff_attn(per_batch(q1), per_batch(q2), per_batch(k_d), per_batch(vt_d), _row(diff_lq1[l]),
                         _row(diff_lk1[l]), _row(diff_lq2[l]), _row(diff_lk2[l]), _col(diff_subln[l]),
                         lambda_init, DIFF_TQ, DIFF_TK)

        w_nsa_p = _pad_cols(w_nsa, NSA_HEADS * LANES + 7 * LANES).astype(BF16)
        q_n, k_n, raw_n, vst_n, vwt_n, gt_n = _nsa_prep(
            h, g_mix, w_nsa_p, _row(nsa_q_norm[l]), _row(nsa_ks_norm[l]), _row(nsa_kw_norm[l]), tm=NSA_TK)
        chunks = lambda z: z.reshape(b, s // NSA_CMP_STRIDE, NSA_CMP_STRIDE * NSA_D)
        kc, vct = _nsa_compress(chunks(raw_n[:, :LANES]), chunks(raw_n[:, LANES:]),
                                nsa_pe_k[l].reshape(1, -1), nsa_pe_v[l].reshape(1, -1),
                                nsa_w_ck[l].astype(BF16), nsa_w_cv[l].astype(BF16), _row(nsa_kc_norm[l]))
        o_c = _nsa_attn(per_batch(q_n), per_batch(gt_n), kc, vct, per_batch(k_n), per_batch(vst_n),
                        per_batch(vwt_n), overlap_t, expand_t, _col(nsa_o_norm[l]), NSA_TQ, NSA_TK)

        qk_sb, vt_sb = _sb_prep(h, g_mix, w_sb.astype(BF16), tm=SB_TK)
        o_d = _sb_attn(per_batch(qk_sb), per_batch(vt_sb), tri, _col(sb_o_norm[l]), SB_TQ, SB_TK)

        flat = lambda a: a.reshape(m, -1)
        h = _out_proj([flat(o_a), flat(o_b), flat(o_c), flat(o_d)], w_out[l].astype(BF16), h)
        h = _ffn(h, ffn2_norm[l], ffn2_w_gate[l].astype(BF16), ffn2_w_up[l].astype(BF16),
                 ffn2_w_down[l].astype(BF16))
    return h.reshape(b, s, d)
```

```python
import functools
import math

import numpy as np
import jax
import jax.numpy as jnp
from jax import lax
from jax.experimental import pallas as pl
from jax.experimental.pallas import tpu as pltpu

F32 = jnp.float32
BF16 = jnp.bfloat16

RMS_EPS = 1e-6
NEG_INF = -1e30
MASK_BIG = 2.0 ** 100
LANES = 128
VMEM_LIMIT = 56 * 1024 * 1024

MLA_HEADS, MLA_Q_RANK, MLA_KV_RANK, MLA_NOPE, MLA_ROPE, MLA_V = 4, 512, 256, 128, 64, 128
ROPE_THETA = 10000.0
DIFF_HEADS, DIFF_QK, DIFF_V = 4, 64, 128
NSA_HEADS, NSA_D = 4, 128
NSA_CMP_LEN, NSA_CMP_STRIDE, NSA_SEL_BLOCK, NSA_N_SELECT, NSA_WINDOW = 32, 16, 64, 16, 512
NSA_FORCE_BONUS = 1e4
SB_HEADS, SB_D = 4, 128
N_ALIBI_HEADS = DIFF_HEADS + NSA_HEADS
ALIBI_SLOPES = [2.0 ** (-8.0 * k / N_ALIBI_HEADS) for k in range(1, N_ALIBI_HEADS + 1)]
LOG2E = math.log2(math.e)
DIFF_SLOPES = [sl * LOG2E for sl in ALIBI_SLOPES[0::2]]
NSA_SLOPES = [sl * LOG2E for sl in ALIBI_SLOPES[1::2]]

W_MLA = MLA_Q_RANK + MLA_KV_RANK + MLA_ROPE
W_DIFF = DIFF_HEADS * (2 * DIFF_QK + 2 * DIFF_QK + DIFF_V)
W_NSA = NSA_HEADS * NSA_D + 6 * NSA_D + NSA_HEADS * 3
W_SB = 3 * SB_HEADS * SB_D

MLA_TQ, MLA_TK = 512, 512
DIFF_TQ, DIFF_TK = 512, 512
SB_TQ, SB_TK = 512, 512
SB_TRI = 256
NSA_TQ, NSA_TK = 128, 512


def _dot(a, b):
    return jnp.dot(a, b, preferred_element_type=F32)


def _dot_nt(a, b):
    return lax.dot_general(a, b, (((1,), (1,)), ((), ())), preferred_element_type=F32)


def _rms_rows(x, g):
    ms = jnp.mean(x * x, axis=-1, keepdims=True)
    return x * lax.rsqrt(ms + RMS_EPS) * g


def _rms_cols(x, g):
    ms = jnp.mean(x * x, axis=0, keepdims=True)
    return x * lax.rsqrt(ms + RMS_EPS) * g


def _params(*sem):
    return pltpu.CompilerParams(dimension_semantics=sem, vmem_limit_bytes=VMEM_LIMIT)


def _full(a, grid_rank):
    return pl.BlockSpec(a.shape, lambda *_: (0,) * a.ndim)


def _ffn_kernel(x_ref, g_ref, wg_ref, wu_ref, wd_ref, o_ref, xn_ref, acc_ref, *, nf):
    f = pl.program_id(1)

    @pl.when(f == 0)
    def _():
        xn_ref[...] = _rms_rows(x_ref[...], g_ref[...]).astype(BF16)
        acc_ref[...] = jnp.zeros_like(acc_ref)

    xn = xn_ref[...]
    gate = _dot(xn, wg_ref[...])
    up = _dot(xn, wu_ref[...])
    mid = (gate * jax.nn.sigmoid(gate) * up).astype(BF16)
    acc_ref[...] += _dot(mid, wd_ref[...])

    @pl.when(f == nf - 1)
    def _():
        o_ref[...] = x_ref[...] + 0.5 * acc_ref[...]


def _ffn(h, g, wg, wu, wd, tm=512, tf=512):
    m, d = h.shape
    f = wg.shape[1]
    return pl.pallas_call(
        functools.partial(_ffn_kernel, nf=f // tf),
        grid=(m // tm, f // tf),
        in_specs=[
            pl.BlockSpec((tm, d), lambda i, j: (i, 0)),
            pl.BlockSpec((1, d), lambda i, j: (0, 0)),
            pl.BlockSpec((d, tf), lambda i, j: (0, j)),
            pl.BlockSpec((d, tf), lambda i, j: (0, j)),
            pl.BlockSpec((tf, d), lambda i, j: (j, 0)),
        ],
        out_specs=pl.BlockSpec((tm, d), lambda i, j: (i, 0)),
        out_shape=jax.ShapeDtypeStruct((m, d), F32),
        scratch_shapes=[pltpu.VMEM((tm, d), BF16), pltpu.VMEM((tm, d), F32)],
        compiler_params=_params("parallel", "arbitrary"),
        name="ffn",
    )(h, g.reshape(1, d), wg, wu, wd)


def _out_proj_kernel(a0_ref, a1_ref, a2_ref, a3_ref, w_ref, r_ref, o_ref):
    acc = r_ref[...]
    for n, a_ref in enumerate((a0_ref, a1_ref, a2_ref, a3_ref)):
        kw = a_ref.shape[1]
        acc = acc + _dot(a_ref[...], w_ref[n * kw:(n + 1) * kw, :])
    o_ref[...] = acc


def _out_proj(parts, w, res, tm=512, tn=1024):
    m, d = res.shape
    kw = parts[0].shape[1]
    a_spec = pl.BlockSpec((tm, kw), lambda i, j: (i, 0))
    return pl.pallas_call(
        _out_proj_kernel,
        grid=(m // tm, d // tn),
        in_specs=[a_spec, a_spec, a_spec, a_spec,
                  pl.BlockSpec((w.shape[0], tn), lambda i, j: (0, j)),
                  pl.BlockSpec((tm, tn), lambda i, j: (i, j))],
        out_specs=pl.BlockSpec((tm, tn), lambda i, j: (i, j)),
        out_shape=jax.ShapeDtypeStruct((m, d), F32),
        compiler_params=_params("parallel", "arbitrary"),
        name="out_proj",
    )(*parts, w, res)


def _sb_prep_kernel(x_ref, g_ref, w_ref, qk_out, vt_out):
    xn = _rms_rows(x_ref[...], g_ref[...]).astype(BF16)
    u = _dot(xn, w_ref[...])
    width = SB_HEADS * SB_D
    qk_out[:, :width] = (u[:, :width] * (SB_D ** -0.5 * LOG2E)).astype(BF16)
    qk_out[:, width:] = u[:, width:2 * width].astype(BF16)
    vt_out[...] = u[:, 2 * width:].T.astype(BF16)


def _sb_prep(h, g, w, tm):
    m, d = h.shape
    width = SB_HEADS * SB_D
    return pl.pallas_call(
        _sb_prep_kernel,
        grid=(m // tm,),
        in_specs=[pl.BlockSpec((tm, d), lambda i: (i, 0)), _full(g, 1), _full(w, 1)],
        out_specs=[pl.BlockSpec((tm, 2 * width), lambda i: (i, 0)),
                   pl.BlockSpec((None, width, tm), lambda i: (i, 0, 0))],
        out_shape=[jax.ShapeDtypeStruct((m, 2 * width), BF16),
                   jax.ShapeDtypeStruct((m // tm, width, tm), BF16)],
        compiler_params=_params("parallel"),
        name="sb_prep",
    )(h, g, w)


def _rope_rows(x, x_sw, g, g_sw, cos2, sin2):
    ms = jnp.sum(x * x, axis=-1, keepdims=True) * (1.0 / MLA_ROPE)
    r = lax.rsqrt(ms + RMS_EPS)
    return (x * r * g) * cos2 + (x_sw * r * g_sw) * sin2


def _mla_prep_kernel(x_ref, g_ref, w_ref, cqn_ref, ckvn_ref, wuq_ref, wukv_ref, qn_ref, qr_ref,
                     qrs_ref, kn_ref, kr_ref, krs_ref, cos_ref, sin_ref, q_out, k_out, vt_out):
    xn = _rms_rows(x_ref[...], g_ref[...]).astype(BF16)
    u = _dot(xn, w_ref[...])
    c_q = u[:, :MLA_Q_RANK]
    c_kv = u[:, MLA_Q_RANK:MLA_Q_RANK + MLA_KV_RANK]
    kr = u[:, MLA_Q_RANK + MLA_KV_RANK:MLA_Q_RANK + MLA_KV_RANK + LANES]
    kr_sw = u[:, MLA_Q_RANK + MLA_KV_RANK + LANES:]
    cos2, sin2 = cos_ref[...], sin_ref[...]
    qall = _dot(_rms_rows(c_q, cqn_ref[...]).astype(BF16), wuq_ref[...])
    kvall = _dot(_rms_rows(c_kv, ckvn_ref[...]).astype(BF16), wukv_ref[...])
    k_rot = _rope_rows(kr, kr_sw, kr_ref[...], krs_ref[...], cos2, sin2).astype(BF16)
    scale = (MLA_NOPE + MLA_ROPE) ** -0.5 * LOG2E
    for h in range(MLA_HEADS):
        qh = qall[:, h * 3 * LANES:(h + 1) * 3 * LANES]
        q_nope = _rms_rows(qh[:, :LANES], qn_ref[...]) * scale
        q_rot = _rope_rows(qh[:, LANES:2 * LANES], qh[:, 2 * LANES:], qr_ref[...], qrs_ref[...],
                           cos2, sin2) * scale
        q_out[:, h * 2 * LANES:h * 2 * LANES + LANES] = q_nope.astype(BF16)
        q_out[:, h * 2 * LANES + LANES:(h + 1) * 2 * LANES] = q_rot.astype(BF16)
        kvh = kvall[:, h * 2 * LANES:(h + 1) * 2 * LANES]
        k_out[:, h * 2 * LANES:h * 2 * LANES + LANES] = _rms_rows(kvh[:, :LANES], kn_ref[...]).astype(BF16)
        k_out[:, h * 2 * LANES + LANES:(h + 1) * 2 * LANES] = k_rot
        vt_out[h * LANES:(h + 1) * LANES, :] = kvh[:, LANES:].T.astype(BF16)


def _mla_prep(h3, g, w, cqn, ckvn, wuq, wukv, qn, qr, qrs, kn, kr, krs, cos2, sin2, tm):
    b, s, d = h3.shape
    smalls = [g, w, cqn, ckvn, wuq, wukv, qn, qr, qrs, kn, kr, krs]
    tab = pl.BlockSpec((tm, LANES), lambda bi, i: (i, 0))
    out = lambda c: pl.BlockSpec((None, tm, c), lambda bi, i: (bi, i, 0))
    return pl.pallas_call(
        _mla_prep_kernel,
        grid=(b, s // tm),
        in_specs=[pl.BlockSpec((None, tm, d), lambda bi, i: (bi, i, 0))] + [_full(a, 2) for a in smalls] + [tab, tab],
        out_specs=[out(MLA_HEADS * 2 * LANES), out(MLA_HEADS * 2 * LANES),
                   pl.BlockSpec((None, None, MLA_HEADS * MLA_V, tm), lambda bi, i: (bi, i, 0, 0))],
        out_shape=[jax.ShapeDtypeStruct((b, s, MLA_HEADS * 2 * LANES), BF16),
                   jax.ShapeDtypeStruct((b, s, MLA_HEADS * 2 * LANES), BF16),
                   jax.ShapeDtypeStruct((b, s // tm, MLA_HEADS * MLA_V, tm), BF16)],
        compiler_params=_params("parallel", "parallel"),
        name="mla_prep",
    )(h3, *smalls, cos2, sin2)


def _diff_prep_kernel(x_ref, g_ref, w_ref, qn_ref, kn_ref, q1_out, q2_out, k_out, vt_out):
    xn = _rms_rows(x_ref[...], g_ref[...]).astype(BF16)
    u = _dot(xn, w_ref[...])
    width = DIFF_HEADS * LANES
    lane = lax.broadcasted_iota(jnp.int32, (1, LANES), 1)
    lo = lane < DIFF_QK
    scale = DIFF_QK ** -0.5 * LOG2E

    def half_rms(x, g2):
        sq = x * x
        s_lo = jnp.sum(jnp.where(lo, sq, 0.0), axis=-1, keepdims=True)
        s_hi = jnp.sum(jnp.where(lo, 0.0, sq), axis=-1, keepdims=True)
        r = jnp.where(lo, lax.rsqrt(s_lo * (1.0 / DIFF_QK) + RMS_EPS),
                      lax.rsqrt(s_hi * (1.0 / DIFF_QK) + RMS_EPS))
        return x * r * g2

    for h in range(DIFF_HEADS):
        sl = slice(h * LANES, (h + 1) * LANES)
        qn = half_rms(u[:, sl], qn_ref[...]) * scale
        q1_out[:, sl] = jnp.where(lo, qn, 0.0).astype(BF16)
        q2_out[:, sl] = jnp.where(lo, 0.0, qn).astype(BF16)
        k_out[:, sl] = half_rms(u[:, width + h * LANES:width + (h + 1) * LANES], kn_ref[...]).astype(BF16)
    vt_out[...] = u[:, 2 * width:].T.astype(BF16)


def _diff_prep(h, g, w, qn2, kn2, tm):
    m, d = h.shape
    width = DIFF_HEADS * LANES
    out = pl.BlockSpec((tm, width), lambda i: (i, 0))
    return pl.pallas_call(
        _diff_prep_kernel,
        grid=(m // tm,),
        in_specs=[pl.BlockSpec((tm, d), lambda i: (i, 0)), _full(g, 1), _full(w, 1), _full(qn2, 1), _full(kn2, 1)],
        out_specs=[out, out, out, pl.BlockSpec((None, width, tm), lambda i: (i, 0, 0))],
        out_shape=[jax.ShapeDtypeStruct((m, width), BF16)] * 3
        + [jax.ShapeDtypeStruct((m // tm, width, tm), BF16)],
        compiler_params=_params("parallel"),
        name="diff_prep",
    )(h, g, w, qn2, kn2)


def _nsa_prep_kernel(x_ref, g_ref, w_ref, qn_ref, ksn_ref, kwn_ref, q_out, k_out, raw_out, vst_out,
                     vwt_out, gt_out):
    xn = _rms_rows(x_ref[...], g_ref[...]).astype(BF16)
    u = _dot(xn, w_ref[...])
    scale = NSA_D ** -0.5 * LOG2E
    for h in range(NSA_HEADS):
        sl = slice(h * LANES, (h + 1) * LANES)
        q_out[:, sl] = (_rms_rows(u[:, sl], qn_ref[...]) * scale).astype(BF16)
    base = NSA_HEADS * LANES
    piece = lambda n: u[:, base + n * LANES:base + (n + 1) * LANES]
    raw_out[:, :LANES] = piece(0)
    raw_out[:, LANES:] = piece(1)
    k_out[:, :LANES] = _rms_rows(piece(2), ksn_ref[...]).astype(BF16)
    k_out[:, LANES:] = _rms_rows(piece(4), kwn_ref[...]).astype(BF16)
    vst_out[...] = piece(3).T.astype(BF16)
    vw_t = piece(5).T.astype(BF16)
    gate_t = jax.nn.sigmoid(piece(6)).T
    for c in range(vwt_out.shape[0]):
        vwt_out[c] = vw_t[:, c * LANES:(c + 1) * LANES]
        gt_out[c] = gate_t[:, c * LANES:(c + 1) * LANES]


def _nsa_prep(h, g, w, qn, ksn, kwn, tm):
    m, d = h.shape
    nblk = tm // LANES
    out = lambda c: pl.BlockSpec((tm, c), lambda i: (i, 0))
    blocks = pl.BlockSpec((nblk, LANES, LANES), lambda i: (i, 0, 0))
    return pl.pallas_call(
        _nsa_prep_kernel,
        grid=(m // tm,),
        in_specs=[pl.BlockSpec((tm, d), lambda i: (i, 0))] + [_full(a, 1) for a in (g, w, qn, ksn, kwn)],
        out_specs=[out(NSA_HEADS * LANES), out(2 * LANES), out(2 * LANES),
                   pl.BlockSpec((None, NSA_D, tm), lambda i: (i, 0, 0)), blocks, blocks],
        out_shape=[jax.ShapeDtypeStruct((m, NSA_HEADS * LANES), BF16),
                   jax.ShapeDtypeStruct((m, 2 * LANES), BF16),
                   jax.ShapeDtypeStruct((m, 2 * LANES), F32),
                   jax.ShapeDtypeStruct((m // tm, NSA_D, tm), BF16),
                   jax.ShapeDtypeStruct((m // LANES, LANES, LANES), BF16),
                   jax.ShapeDtypeStruct((m // LANES, LANES, LANES), F32)],
        compiler_params=_params("parallel"),
        name="nsa_prep",
    )(h, g, w, qn, ksn, kwn)


def _nsa_compress_kernel(zk_ref, zv_ref, pek_ref, pev_ref, wk_ref, wv_ref, kcn_ref, kc_out, vct_out):
    half = NSA_CMP_STRIDE * NSA_D
    nchunk = zk_ref.shape[0]

    def compress(z, pe_ref, w_ref):
        first = _dot((z + pe_ref[:, :half]).astype(BF16), w_ref[:half, :])
        second = _dot((z + pe_ref[:, half:]).astype(BF16), w_ref[half:, :])
        return first + pltpu.roll(second, shift=nchunk - 1, axis=0)

    kc_out[...] = _rms_rows(compress(zk_ref[...], pek_ref, wk_ref), kcn_ref[...]).astype(BF16)
    vct_out[...] = compress(zv_ref[...], pev_ref, wv_ref).T.astype(BF16)


def _nsa_compress(zk, zv, pek, pev, wk, wv, kcn):
    b, nchunk, width = zk.shape
    z_spec = pl.BlockSpec((None, nchunk, width), lambda bi: (bi, 0, 0))
    return pl.pallas_call(
        _nsa_compress_kernel,
        grid=(b,),
        in_specs=[z_spec, z_spec] + [_full(a, 1) for a in (pek, pev, wk, wv, kcn)],
        out_specs=[pl.BlockSpec((None, nchunk, NSA_D), lambda bi: (bi, 0, 0)),
                   pl.BlockSpec((None, NSA_D, nchunk), lambda bi: (bi, 0, 0))],
        out_shape=[jax.ShapeDtypeStruct((b, nchunk, NSA_D), BF16),
                   jax.ShapeDtypeStruct((b, NSA_D, nchunk), BF16)],
        compiler_params=_params("parallel"),
        name="nsa_compress",
    )(zk, zv, pek, pev, wk, wv, kcn)


def _softmax_step(s, vt, m, l, acc):
    m_new = jnp.maximum(m, jnp.max(s, axis=0, keepdims=True))
    alpha = jnp.exp2(m - m_new)
    p = jnp.exp2(s - m_new)
    l = alpha * l + jnp.sum(p, axis=0, keepdims=True)
    acc = alpha * acc + _dot(vt, p.astype(BF16))
    return m_new, l, acc


def _softmax_init(nq, dv):
    return (jnp.full((1, nq), NEG_INF, F32), jnp.zeros((1, nq), F32), jnp.zeros((dv, nq), F32))


def _softmax_out(l, acc):
    return acc * (1.0 / jnp.maximum(l, 1e-30))


def _softmax_exact(s, mask):
    m = jnp.max(s, axis=0, keepdims=True)
    e = jnp.where(mask, jnp.exp2(s - m), 0.0)
    return e * (1.0 / jnp.maximum(jnp.sum(e, axis=0, keepdims=True), 1e-30))


def _causal_mask(tk, tq, d, strict=False, reps=1):
    key = d * tk + lax.broadcasted_iota(jnp.int32, (tk, reps * tq), 0)
    qry = lax.broadcasted_iota(jnp.int32, (tk, reps * tq), 1)
    for _ in range(1, reps):
        qry = jnp.where(qry >= tq, qry - tq, qry)
    return key < qry if strict else key <= qry


def _lanes(parts):
    return jnp.concatenate(parts, axis=1)


def _mla_attn_kernel(q_ref, k_ref, vt_ref, on_ref, o_ref, *, tq, tk):
    i = pl.program_id(1)
    r = tq // tk
    dq = 2 * LANES
    qs = [q_ref[:, h * dq:(h + 1) * dq] for h in range(MLA_HEADS)]

    def step(j, carry, mask):
        st = pl.multiple_of(j * tk, tk)
        scores = lambda h: _dot_nt(k_ref[pl.ds(st, tk), h * dq:(h + 1) * dq], qs[h])
        out = ()
        s_next = scores(0)
        for h in range(MLA_HEADS):
            s, s_next = s_next, (scores(h + 1) if h + 1 < MLA_HEADS else None)
            if mask is not None:
                s = jnp.where(mask, s, NEG_INF)
            out += _softmax_step(s, vt_ref[j, h * MLA_V:(h + 1) * MLA_V, :], *carry[3 * h:3 * h + 3])
        return out

    c = lax.fori_loop(0, i * r, lambda j, c: step(j, c, None), _softmax_init(tq, MLA_V) * MLA_HEADS)
    for d in range(r):
        c = step(i * r + d, c, _causal_mask(tk, tq, d))
    for h in range(MLA_HEADS):
        o = _softmax_out(c[3 * h + 1], c[3 * h + 2])
        o_ref[:, h * MLA_V:(h + 1) * MLA_V] = _rms_cols(o, on_ref[...]).T.astype(o_ref.dtype)


def _mla_attn(q, k, vt, on, tq, tk):
    b, s, _ = q.shape
    return pl.pallas_call(
        functools.partial(_mla_attn_kernel, tq=tq, tk=tk),
        grid=(b, s // tq),
        in_specs=[pl.BlockSpec((None, tq, q.shape[2]), lambda bi, i: (bi, i, 0)),
                  pl.BlockSpec((None, s, k.shape[2]), lambda bi, i: (bi, 0, 0)),
                  pl.BlockSpec((None,) + vt.shape[1:], lambda bi, i: (bi, 0, 0, 0)),
                  _full(on, 2)],
        out_specs=pl.BlockSpec((None, tq, MLA_HEADS * MLA_V), lambda bi, i: (bi, i, 0)),
        out_shape=jax.ShapeDtypeStruct((b, s, MLA_HEADS * MLA_V), BF16),
        compiler_params=_params("parallel", "arbitrary"),
        name="mla_attn",
    )(q, k, vt, on)


def _diff_attn_kernel(q1_ref, q2_ref, k_ref, vt_ref, lq1_ref, lk1_ref, lq2_ref, lk2_ref,
                      sub_ref, o_ref, *, tq, tk, lambda_init):
    i = pl.program_id(1)
    r = tq // tk
    t0 = i * tq
    keyrel = lax.broadcasted_iota(jnp.int32, (tk, 1), 0)
    qs = [jnp.concatenate([q1_ref[:, h * LANES:(h + 1) * LANES], q2_ref[:, h * LANES:(h + 1) * LANES]], axis=0)
          for h in range(DIFF_HEADS)]

    def step(j, carry, mask):
        st = pl.multiple_of(j * tk, tk)
        rel = (keyrel + (j * tk - t0)).astype(F32)
        scores = lambda h: _dot_nt(k_ref[pl.ds(st, tk), h * LANES:(h + 1) * LANES], qs[h])
        out = ()
        s_next = scores(0)
        for h in range(DIFF_HEADS):
            sl = slice(h * LANES, (h + 1) * LANES)
            s, s_next = s_next, (scores(h + 1) if h + 1 < DIFF_HEADS else None)
            s = s + DIFF_SLOPES[h] * rel
            if mask is not None:
                s = jnp.where(mask, s, NEG_INF)
            out += _softmax_step(s, vt_ref[j, sl, :], *carry[3 * h:3 * h + 3])
        return out

    c = lax.fori_loop(0, i * r, lambda j, c: step(j, c, None), _softmax_init(2 * tq, DIFF_V) * DIFF_HEADS)
    for d in range(r):
        c = step(i * r + d, c, _causal_mask(tk, tq, d, reps=2))
    lam = (jnp.exp(jnp.sum(lq1_ref[...] * lk1_ref[...], axis=-1, keepdims=True))
           - jnp.exp(jnp.sum(lq2_ref[...] * lk2_ref[...], axis=-1, keepdims=True)) + lambda_init)
    for h in range(DIFF_HEADS):
        o12 = _softmax_out(c[3 * h + 1], c[3 * h + 2])
        o = o12[:, :tq] - lam * o12[:, tq:]
        o_ref[:, h * DIFF_V:(h + 1) * DIFF_V] = (
            _rms_cols(o, sub_ref[...]) * (1.0 - lambda_init)).T.astype(o_ref.dtype)


def _diff_attn(q1, q2, k, vt, lq1, lk1, lq2, lk2, sub, lambda_init, tq, tk):
    b, s, w = q1.shape
    qs = pl.BlockSpec((None, tq, w), lambda bi, i: (bi, i, 0))
    smalls = [lq1, lk1, lq2, lk2, sub]
    return pl.pallas_call(
        functools.partial(_diff_attn_kernel, tq=tq, tk=tk, lambda_init=lambda_init),
        grid=(b, s // tq),
        in_specs=[qs, qs, pl.BlockSpec((None, s, w), lambda bi, i: (bi, 0, 0)),
                  pl.BlockSpec((None,) + vt.shape[1:], lambda bi, i: (bi, 0, 0, 0))]
        + [_full(a, 2) for a in smalls],
        out_specs=pl.BlockSpec((None, tq, DIFF_HEADS * DIFF_V), lambda bi, i: (bi, i, 0)),
        out_shape=jax.ShapeDtypeStruct((b, s, DIFF_HEADS * DIFF_V), BF16),
        compiler_params=_params("parallel", "arbitrary"),
        name="diff_attn",
    )(q1, q2, k, vt, *smalls)


def _sb_attn_kernel(q_ref, k_ref, vt_ref, tri_ref, on_ref, o_ref, *, tq, tk):
    i = pl.program_id(1)
    r = tq // tk
    tri = tri_ref[...]
    tb = tri.shape[0]
    qs = [q_ref[:, h * SB_D:(h + 1) * SB_D] for h in range(SB_HEADS)]

    def step(j, carry, mask):
        st = pl.multiple_of(j * tk, tk)
        logits = lambda h: _dot_nt(k_ref[pl.ds(st, tk), h * SB_D:(h + 1) * SB_D], qs[h])
        out = ()
        z_next = logits(0)
        for h in range(SB_HEADS):
            run, acc = carry[2 * h:2 * h + 2]
            sl = slice(h * SB_D, (h + 1) * SB_D)
            z, z_next = z_next, (logits(h + 1) if h + 1 < SB_HEADS else None)
            log_sig = jnp.minimum(z, 0.0) - jnp.log2(1.0 + jnp.exp2(-jnp.abs(z)))
            log_not = log_sig - z
            if mask is not None:
                log_not = jnp.where(mask, log_not, 0.0)
            hi = log_not.astype(BF16)
            lo = (log_not - hi.astype(F32)).astype(BF16)
            after = []
            for blk in reversed(range(tk // tb)):
                rows = slice(blk * tb, (blk + 1) * tb)
                after.insert(0, _dot(tri, hi[rows]) + _dot(tri, lo[rows]) + run)
                run = run + jnp.sum(log_not[rows], axis=0, keepdims=True)
            a = jnp.exp2(log_sig + jnp.concatenate(after, axis=0))
            if mask is not None:
                a = jnp.where(mask, a, 0.0)
            acc = acc + _dot(vt_ref[j, sl, :], a.astype(BF16))
            out += (run, acc)
        return out

    carry = (jnp.zeros((1, tq), F32), jnp.zeros((SB_D, tq), F32)) * SB_HEADS
    for d in reversed(range(r)):
        carry = step(i * r + d, carry, _causal_mask(tk, tq, d, strict=True))
    carry = lax.fori_loop(0, i * r, lambda n, c: step(i * r - 1 - n, c, None), carry)
    for h in range(SB_HEADS):
        o_ref[:, h * SB_D:(h + 1) * SB_D] = _rms_cols(carry[2 * h + 1], on_ref[...]).T.astype(o_ref.dtype)


def _sb_attn(qk, vt, tri, on, tq, tk):
    b, s, _ = qk.shape
    w = SB_HEADS * SB_D
    return pl.pallas_call(
        functools.partial(_sb_attn_kernel, tq=tq, tk=tk),
        grid=(b, s // tq),
        in_specs=[pl.BlockSpec((None, tq, w), lambda bi, i: (bi, i, 0)),
                  pl.BlockSpec((None, s, w), lambda bi, i: (bi, 0, 1)),
                  pl.BlockSpec((None,) + vt.shape[1:], lambda bi, i: (bi, 0, 0, 0)),
                  _full(tri, 2), _full(on, 2)],
        out_specs=pl.BlockSpec((None, tq, w), lambda bi, i: (bi, i, 0)),
        out_shape=jax.ShapeDtypeStruct((b, s, w), BF16),
        compiler_params=_params("parallel", "arbitrary"),
        name="sb_attn",
    )(qk, qk, vt, tri, on)


def _topk_mask(score, n):
    blk = lax.broadcasted_iota(jnp.int32, score.shape, 0).astype(F32)

    def body(_, c):
        work, sel = c
        mx = jnp.max(work, axis=0, keepdims=True)
        idx = jnp.min(jnp.where(work == mx, blk, 1e9), axis=0, keepdims=True)
        pick = blk == idx
        return jnp.where(pick, -jnp.inf, work), jnp.where(pick, 1.0, sel)

    _, sel = lax.fori_loop(0, n, body, (score, jnp.zeros_like(score)), unroll=True)
    return sel


def _nsa_attn_kernel(q_ref, gt_ref, kc_ref, vct_ref, ks_ref, kw_ref, vst_ref, vwt_ref, ovt_ref, ext_ref,
                     on_ref, o_ref, selb_ref, *, t, tk, n_sb, n_sel):
    i = pl.program_id(1)
    t0 = i * t
    ncp = kc_ref.shape[0]
    nh = NSA_HEADS
    q4 = jnp.concatenate([q_ref[:, h * LANES:(h + 1) * LANES] for h in range(nh)], axis=0)
    qpos1 = t0 + lax.broadcasted_iota(jnp.int32, (1, t), 1)
    qpos4 = _lanes([qpos1] * nh)
    slope4 = _lanes([jnp.full((1, t), NSA_SLOPES[h], F32) for h in range(nh)])

    c_end = NSA_CMP_STRIDE * lax.broadcasted_iota(jnp.int32, (ncp, 1), 0) + (NSA_CMP_LEN - 1)
    cmask = c_end <= qpos4
    s = jnp.where(cmask, _dot_nt(kc_ref[...], q4) + slope4 * (c_end - t0).astype(F32), NEG_INF)
    p = _softmax_exact(s, cmask)
    o_cmp = _dot(vct_ref[...], p.astype(BF16))
    pc_sum = p[:, 0:t]
    for h in range(1, nh):
        pc_sum = pc_sum + p[:, h * t:(h + 1) * t]
    ovt = ovt_ref[...]
    hi = pc_sum.astype(BF16)
    r1 = pc_sum - hi.astype(F32)
    mid = r1.astype(BF16)
    lo = (r1 - mid.astype(F32)).astype(BF16)
    imp = _dot(ovt, hi) + _dot(ovt, mid) + _dot(ovt, lo)

    jb = lax.broadcasted_iota(jnp.int32, (LANES, 1), 0)
    cur = qpos1 >> int(math.log2(NSA_SEL_BLOCK))
    valid = (jb * NSA_SEL_BLOCK <= qpos1) & (jb < n_sb)
    forced = (jb == 0) | (jb == cur) | (jb == cur - 1)
    score = jnp.where(valid, imp + jnp.where(forced, NSA_FORCE_BONUS, 0.0), NEG_INF)
    unselected = (jnp.where(score > 0.5 * NEG_INF, _topk_mask(score, n_sel), 0.0) - 1.0).astype(BF16)
    for jj in range(selb_ref.shape[0]):
        selb_ref[jj] = _dot(ext_ref[jj * tk:(jj + 1) * tk, :], unselected)

    nwin = NSA_WINDOW // t + 1
    wblk = jnp.maximum(i - NSA_WINDOW // t, 0)
    w0 = pl.multiple_of(wblk * t, t)
    wcol = w0 + lax.broadcasted_iota(jnp.int32, (nwin * t, 1), 0)
    dist = qpos4 - wcol
    wmask = (dist >= 0) & (dist < NSA_WINDOW)
    s = _dot_nt(kw_ref[pl.ds(w0, nwin * t), :], q4) + slope4 * (wcol - t0).astype(F32)
    p = _softmax_exact(jnp.where(wmask, s, NEG_INF), wmask).astype(BF16)
    o_win = _dot(vwt_ref[wblk], p[0:t])
    for c in range(1, nwin):
        o_win = o_win + _dot(vwt_ref[wblk + c], p[c * t:(c + 1) * t])

    def sel_step(j, c, causal):
        st = pl.multiple_of(j * tk, tk)
        col = st + lax.broadcasted_iota(jnp.int32, (tk, 1), 0)
        s = (_dot_nt(ks_ref[pl.ds(st, tk), :], q4) + slope4 * (col - t0).astype(F32)
             + _lanes([selb_ref[j]] * nh))
        if causal:
            s = jnp.where(col <= qpos4, s, NEG_INF)
        return _softmax_step(s, vst_ref[j], *c)

    n_full = t0 // tk
    c = lax.fori_loop(0, n_full, lambda j, c: sel_step(j, c, False), _softmax_init(nh * t, NSA_D))
    _, l, acc = sel_step(n_full, c, True)
    o_sel = _softmax_out(l, acc)

    gates = gt_ref[...]
    for h in range(nh):
        cols = slice(h * t, (h + 1) * t)
        o = (gates[3 * h:3 * h + 1] * o_cmp[:, cols] + gates[3 * h + 1:3 * h + 2] * o_sel[:, cols]
             + gates[3 * h + 2:3 * h + 3] * o_win[:, cols])
        o_ref[:, h * LANES:(h + 1) * LANES] = _rms_cols(o, on_ref[...]).T.astype(o_ref.dtype)


def _nsa_attn(q, gt, kc, vct, kk, vst, vwt, ovt, ext, on, t, tk):
    b, s, _ = q.shape
    ncp = kc.shape[1]
    n_sb = s // NSA_SEL_BLOCK
    n_sel = min(NSA_N_SELECT, n_sb)
    k_spec = lambda n: pl.BlockSpec((None, s, NSA_D), lambda bi, i: (bi, 0, n))
    whole = lambda a: pl.BlockSpec((None,) + a.shape[1:], lambda bi, i: (bi,) + (0,) * (a.ndim - 1))
    return pl.pallas_call(
        functools.partial(_nsa_attn_kernel, t=t, tk=tk, n_sb=n_sb, n_sel=n_sel),
        grid=(b, s // t),
        in_specs=[pl.BlockSpec((None, t, NSA_HEADS * NSA_D), lambda bi, i: (bi, i, 0)),
                  pl.BlockSpec((None, None, LANES, t), lambda bi, i: (bi, i, 0, 0)),
                  whole(kc), whole(vct), k_spec(0), k_spec(1), whole(vst), whole(vwt),
                  _full(ovt, 2), _full(ext, 2), _full(on, 2)],
        out_specs=pl.BlockSpec((None, t, NSA_HEADS * NSA_D), lambda bi, i: (bi, i, 0)),
        out_shape=jax.ShapeDtypeStruct((b, s, NSA_HEADS * NSA_D), BF16),
        scratch_shapes=[pltpu.VMEM((s // tk, tk, t), F32)],
        compiler_params=_params("parallel", "arbitrary"),
        name="nsa_attn",
    )(q, gt, kc, vct, kk, kk, vst, vwt, ovt, ext, on)


def _pad_cols(w, width):
    return jnp.pad(w, ((0, 0), (0, width - w.shape[1])))


def _swap_halves(w):
    half = w.shape[-1] // 2
    return jnp.concatenate([w[..., half:], w[..., :half]], axis=-1)


def _row(v):
    return v.reshape(1, -1).astype(F32)


def _col(v):
    return v.reshape(-1, 1).astype(F32)


def _nsa_constants(s):
    ncp = s // NSA_CMP_STRIDE
    n_sb = s // NSA_SEL_BLOCK
    c_start = NSA_CMP_STRIDE * np.arange(ncp)[None, :]
    sel_start = NSA_SEL_BLOCK * np.arange(LANES)[:, None]
    overlap_t = ((c_start < sel_start + NSA_SEL_BLOCK) & (c_start + NSA_CMP_LEN > sel_start)
                 & (np.arange(LANES)[:, None] < n_sb))
    expand_t = (np.arange(s)[:, None] // NSA_SEL_BLOCK) == np.arange(LANES)[None, :]
    return jnp.asarray(overlap_t, BF16), jnp.asarray(expand_t * MASK_BIG, BF16)


def kernel(x, ffn1_norm, ffn1_w_gate, ffn1_w_up, ffn1_w_down, mix_norm, w_in, mla_cq_norm, mla_ckv_norm, mla_w_uq, mla_w_ukv, mla_qn_norm, mla_qr_norm, mla_kn_norm, mla_kr_norm, mla_o_norm, diff_q_norm, diff_k_norm, diff_lq1, diff_lk1, diff_lq2, diff_lk2, diff_subln, nsa_q_norm, nsa_pe_k, nsa_w_ck, nsa_pe_v, nsa_w_cv, nsa_kc_norm, nsa_ks_norm, nsa_kw_norm, nsa_o_norm, sb_o_norm, w_out, ffn2_norm, ffn2_w_gate, ffn2_w_up, ffn2_w_down):
    b, s, d = x.shape
    m = b * s
    depth = w_in.shape[0]

    pos = jnp.arange(s, dtype=F32)
    inv_freq = ROPE_THETA ** (-jnp.arange(0, MLA_ROPE, 2, dtype=F32) / MLA_ROPE)
    ang = pos[:, None] * inv_freq[None, :]
    cos, sin = jnp.cos(ang), jnp.sin(ang)
    cos2 = _pad_cols(jnp.concatenate([cos, cos], axis=-1), LANES)
    sin2 = _pad_cols(jnp.concatenate([-sin, sin], axis=-1), LANES)
    tri = jnp.asarray(np.arange(SB_TRI)[None, :] > np.arange(SB_TRI)[:, None], BF16)
    overlap_t, expand_t = _nsa_constants(s)

    h = x.reshape(m, d)
    for l in range(depth):
        lambda_init = 0.8 - 0.6 * math.exp(-0.3 * l)
        h = _ffn(h, ffn1_norm[l], ffn1_w_gate[l].astype(BF16), ffn1_w_up[l].astype(BF16),
                 ffn1_w_down[l].astype(BF16))

        g_mix = _row(mix_norm[l])
        wl = w_in[l]
        o0 = 0
        w_mla, o0 = wl[:, o0:o0 + W_MLA], o0 + W_MLA
        w_diff, o0 = wl[:, o0:o0 + W_DIFF], o0 + W_DIFF
        w_nsa, o0 = wl[:, o0:o0 + W_NSA], o0 + W_NSA
        w_sb = wl[:, o0:o0 + W_SB]
        per_batch = lambda a: a.reshape((b, a.shape[0] // b) + a.shape[1:])

        w_kr = w_mla[:, MLA_Q_RANK + MLA_KV_RANK:]
        w_mla_p = jnp.concatenate([w_mla[:, :MLA_Q_RANK + MLA_KV_RANK], _pad_cols(w_kr, LANES),
                                   _pad_cols(_swap_halves(w_kr), LANES)], axis=1).astype(BF16)
        wuq = mla_w_uq[l].reshape(MLA_Q_RANK, MLA_HEADS, MLA_NOPE + MLA_ROPE)
        wuq_r = wuq[..., MLA_NOPE:]
        pad3 = lambda a: jnp.pad(a, ((0, 0), (0, 0), (0, LANES - a.shape[-1])))
        wuq_p = jnp.concatenate([wuq[..., :MLA_NOPE], pad3(wuq_r), pad3(_swap_halves(wuq_r))],
                                axis=-1).reshape(MLA_Q_RANK, MLA_HEADS * 3 * LANES).astype(BF16)
        padr = lambda v: _pad_cols(_row(v), LANES)
        q_mla, k_mla, vt_mla = _mla_prep(
            h.reshape(b, s, d), g_mix, w_mla_p, _row(mla_cq_norm[l]), _row(mla_ckv_norm[l]), wuq_p,
            mla_w_ukv[l].astype(BF16), _row(mla_qn_norm[l]), padr(mla_qr_norm[l]),
            padr(_swap_halves(mla_qr_norm[l])), _row(mla_kn_norm[l]), padr(mla_kr_norm[l]),
            padr(_swap_halves(mla_kr_norm[l])), cos2, sin2, tm=MLA_TK)
        o_a = _mla_attn(q_mla, k_mla, vt_mla, _col(mla_o_norm[l]), MLA_TQ, MLA_TK)

        two = lambda v: jnp.concatenate([_row(v), _row(v)], axis=1)
        q1, q2, k_d, vt_d = _diff_prep(h, g_mix, w_diff.astype(BF16), two(diff_q_norm[l]),
                                       two(diff_k_norm[l]), tm=DIFF_TK)
        o_b = _diff_attn(per_batch(q1), per_batch(q2), per_batch(k_d), per_batch(vt_d), _row(diff_lq1[l]),
                         _row(diff_lk1[l]), _row(diff_lq2[l]), _row(diff_lk2[l]), _col(diff_subln[l]),
                         lambda_init, DIFF_TQ, DIFF_TK)

        w_nsa_p = _pad_cols(w_nsa, NSA_HEADS * LANES + 7 * LANES).astype(BF16)
        q_n, k_n, raw_n, vst_n, vwt_n, gt_n = _nsa_prep(
            h, g_mix, w_nsa_p, _row(nsa_q_norm[l]), _row(nsa_ks_norm[l]), _row(nsa_kw_norm[l]), tm=NSA_TK)
        chunks = lambda z: z.reshape(b, s // NSA_CMP_STRIDE, NSA_CMP_STRIDE * NSA_D)
        kc, vct = _nsa_compress(chunks(raw_n[:, :LANES]), chunks(raw_n[:, LANES:]),
                                nsa_pe_k[l].reshape(1, -1), nsa_pe_v[l].reshape(1, -1),
                                nsa_w_ck[l].astype(BF16), nsa_w_cv[l].astype(BF16), _row(nsa_kc_norm[l]))
        o_c = _nsa_attn(per_batch(q_n), per_batch(gt_n), kc, vct, per_batch(k_n), per_batch(vst_n),
                        per_batch(vwt_n), overlap_t, expand_t, _col(nsa_o_norm[l]), NSA_TQ, NSA_TK)

        qk_sb, vt_sb = _sb_prep(h, g_mix, w_sb.astype(BF16), tm=SB_TK)
        o_d = _sb_attn(per_batch(qk_sb), per_batch(vt_sb), tri, _col(sb_o_norm[l]), SB_TQ, SB_TK)

        flat = lambda a: a.reshape(m, -1)
        h = _out_proj([flat(o_a), flat(o_b), flat(o_c), flat(o_d)], w_out[l].astype(BF16), h)
        h = _ffn(h, ffn2_norm[l], ffn2_w_gate[l].astype(BF16), ffn2_w_up[l].astype(BF16),
                 ffn2_w_down[l].astype(BF16))
    return h.reshape(b, s, d)
```

```python
import functools
import math

import numpy as np
import jax
import jax.numpy as jnp
from jax import lax
from jax.experimental import pallas as pl
from jax.experimental.pallas import tpu as pltpu

F32 = jnp.float32
BF16 = jnp.bfloat16

RMS_EPS = 1e-6
NEG_INF = -1e30
MASK_BIG = 2.0 ** 100
LANES = 128
VMEM_LIMIT = 56 * 1024 * 1024

MLA_HEADS, MLA_Q_RANK, MLA_KV_RANK, MLA_NOPE, MLA_ROPE, MLA_V = 4, 512, 256, 128, 64, 128
ROPE_THETA = 10000.0
DIFF_HEADS, DIFF_QK, DIFF_V = 4, 64, 128
NSA_HEADS, NSA_D = 4, 128
NSA_CMP_LEN, NSA_CMP_STRIDE, NSA_SEL_BLOCK, NSA_N_SELECT, NSA_WINDOW = 32, 16, 64, 16, 512
NSA_FORCE_BONUS = 1e4
SB_HEADS, SB_D = 4, 128
N_ALIBI_HEADS = DIFF_HEADS + NSA_HEADS
ALIBI_SLOPES = [2.0 ** (-8.0 * k / N_ALIBI_HEADS) for k in range(1, N_ALIBI_HEADS + 1)]
LOG2E = math.log2(math.e)
DIFF_SLOPES = [sl * LOG2E for sl in ALIBI_SLOPES[0::2]]
NSA_SLOPES = [sl * LOG2E for sl in ALIBI_SLOPES[1::2]]

W_MLA = MLA_Q_RANK + MLA_KV_RANK + MLA_ROPE
W_DIFF = DIFF_HEADS * (2 * DIFF_QK + 2 * DIFF_QK + DIFF_V)
W_NSA = NSA_HEADS * NSA_D + 6 * NSA_D + NSA_HEADS * 3
W_SB = 3 * SB_HEADS * SB_D

MLA_TQ, MLA_TK = 512, 512
DIFF_TQ, DIFF_TK = 512, 512
SB_TQ, SB_TK = 512, 512
SB_TRI = 256
NSA_TQ, NSA_TK = 128, 512


def _dot(a, b):
    return jnp.dot(a, b, preferred_element_type=F32)


def _dot_nt(a, b):
    return lax.dot_general(a, b, (((1,), (1,)), ((), ())), preferred_element_type=F32)


def _rms_rows(x, g):
    ms = jnp.mean(x * x, axis=-1, keepdims=True)
    return x * lax.rsqrt(ms + RMS_EPS) * g


def _rms_cols(x, g):
    ms = jnp.mean(x * x, axis=0, keepdims=True)
    return x * lax.rsqrt(ms + RMS_EPS) * g


def _params(*sem):
    return pltpu.CompilerParams(dimension_semantics=sem, vmem_limit_bytes=VMEM_LIMIT)


def _full(a, grid_rank):
    return pl.BlockSpec(a.shape, lambda *_: (0,) * a.ndim)


def _ffn_kernel(x_ref, g_ref, wg_ref, wu_ref, wd_ref, o_ref, xn_ref, acc_ref, *, nf):
    f = pl.program_id(1)

    @pl.when(f == 0)
    def _():
        xn_ref[...] = _rms_rows(x_ref[...], g_ref[...]).astype(BF16)
        acc_ref[...] = jnp.zeros_like(acc_ref)

    xn = xn_ref[...]
    gate = _dot(xn, wg_ref[...])
    up = _dot(xn, wu_ref[...])
    mid = (gate * jax.nn.sigmoid(gate) * up).astype(BF16)
    acc_ref[...] += _dot(mid, wd_ref[...])

    @pl.when(f == nf - 1)
    def _():
        o_ref[...] = x_ref[...] + 0.5 * acc_ref[...]


def _ffn(h, g, wg, wu, wd, tm=512, tf=512):
    m, d = h.shape
    f = wg.shape[1]
    return pl.pallas_call(
        functools.partial(_ffn_kernel, nf=f // tf),
        grid=(m // tm, f // tf),
        in_specs=[
            pl.BlockSpec((tm, d), lambda i, j: (i, 0)),
            pl.BlockSpec((1, d), lambda i, j: (0, 0)),
            pl.BlockSpec((d, tf), lambda i, j: (0, j)),
            pl.BlockSpec((d, tf), lambda i, j: (0, j)),
            pl.BlockSpec((tf, d), lambda i, j: (j, 0)),
        ],
        out_specs=pl.BlockSpec((tm, d), lambda i, j: (i, 0)),
        out_shape=jax.ShapeDtypeStruct((m, d), F32),
        scratch_shapes=[pltpu.VMEM((tm, d), BF16), pltpu.VMEM((tm, d), F32)],
        compiler_params=_params("parallel", "arbitrary"),
        name="ffn",
    )(h, g.reshape(1, d), wg, wu, wd)


def _out_proj_kernel(a0_ref, a1_ref, a2_ref, a3_ref, w_ref, r_ref, o_ref):
    acc = r_ref[...]
    for n, a_ref in enumerate((a0_ref, a1_ref, a2_ref, a3_ref)):
        kw = a_ref.shape[1]
        acc = acc + _dot(a_ref[...], w_ref[n * kw:(n + 1) * kw, :])
    o_ref[...] = acc


def _out_proj(parts, w, res, tm=512, tn=2048):
    m, d = res.shape
    kw = parts[0].shape[1]
    a_spec = pl.BlockSpec((tm, kw), lambda i, j: (i, 0))
    return pl.pallas_call(
        _out_proj_kernel,
        grid=(m // tm, d // tn),
        in_specs=[a_spec, a_spec, a_spec, a_spec,
                  pl.BlockSpec((w.shape[0], tn), lambda i, j: (0, j)),
                  pl.BlockSpec((tm, tn), lambda i, j: (i, j))],
        out_specs=pl.BlockSpec((tm, tn), lambda i, j: (i, j)),
        out_shape=jax.ShapeDtypeStruct((m, d), F32),
        compiler_params=_params("parallel", "arbitrary"),
        name="out_proj",
    )(*parts, w, res)


def _sb_prep_kernel(x_ref, g_ref, w_ref, qk_out, vt_out):
    xn = _rms_rows(x_ref[...], g_ref[...]).astype(BF16)
    u = _dot(xn, w_ref[...])
    width = SB_HEADS * SB_D
    qk_out[:, :width] = (u[:, :width] * (SB_D ** -0.5 * LOG2E)).astype(BF16)
    qk_out[:, width:] = u[:, width:2 * width].astype(BF16)
    vt_out[...] = u[:, 2 * width:].T.astype(BF16)


def _sb_prep(h, g, w, tm):
    m, d = h.shape
    width = SB_HEADS * SB_D
    return pl.pallas_call(
        _sb_prep_kernel,
        grid=(m // tm,),
        in_specs=[pl.BlockSpec((tm, d), lambda i: (i, 0)), _full(g, 1), _full(w, 1)],
        out_specs=[pl.BlockSpec((tm, 2 * width), lambda i: (i, 0)),
                   pl.BlockSpec((None, width, tm), lambda i: (i, 0, 0))],
        out_shape=[jax.ShapeDtypeStruct((m, 2 * width), BF16),
                   jax.ShapeDtypeStruct((m // tm, width, tm), BF16)],
        compiler_params=_params("parallel"),
        name="sb_prep",
    )(h, g, w)


def _rope_rows(x, x_sw, g, g_sw, cos2, sin2):
    ms = jnp.sum(x * x, axis=-1, keepdims=True) * (1.0 / MLA_ROPE)
    r = lax.rsqrt(ms + RMS_EPS)
    return (x * r * g) * cos2 + (x_sw * r * g_sw) * sin2


def _mla_prep_kernel(x_ref, g_ref, w_ref, cqn_ref, ckvn_ref, wuq_ref, wukv_ref, qn_ref, qr_ref,
                     qrs_ref, kn_ref, kr_ref, krs_ref, cos_ref, sin_ref, q_out, k_out, vt_out):
    xn = _rms_rows(x_ref[...], g_ref[...]).astype(BF16)
    u = _dot(xn, w_ref[...])
    c_q = u[:, :MLA_Q_RANK]
    c_kv = u[:, MLA_Q_RANK:MLA_Q_RANK + MLA_KV_RANK]
    kr = u[:, MLA_Q_RANK + MLA_KV_RANK:MLA_Q_RANK + MLA_KV_RANK + LANES]
    kr_sw = u[:, MLA_Q_RANK + MLA_KV_RANK + LANES:]
    cos2, sin2 = cos_ref[...], sin_ref[...]
    qall = _dot(_rms_rows(c_q, cqn_ref[...]).astype(BF16), wuq_ref[...])
    kvall = _dot(_rms_rows(c_kv, ckvn_ref[...]).astype(BF16), wukv_ref[...])
    k_rot = _rope_rows(kr, kr_sw, kr_ref[...], krs_ref[...], cos2, sin2).astype(BF16)
    scale = (MLA_NOPE + MLA_ROPE) ** -0.5 * LOG2E
    for h in range(MLA_HEADS):
        qh = qall[:, h * 3 * LANES:(h + 1) * 3 * LANES]
        q_nope = _rms_rows(qh[:, :LANES], qn_ref[...]) * scale
        q_rot = _rope_rows(qh[:, LANES:2 * LANES], qh[:, 2 * LANES:], qr_ref[...], qrs_ref[...],
                           cos2, sin2) * scale
        q_out[:, h * 2 * LANES:h * 2 * LANES + LANES] = q_nope.astype(BF16)
        q_out[:, h * 2 * LANES + LANES:(h + 1) * 2 * LANES] = q_rot.astype(BF16)
        kvh = kvall[:, h * 2 * LANES:(h + 1) * 2 * LANES]
        k_out[:, h * 2 * LANES:h * 2 * LANES + LANES] = _rms_rows(kvh[:, :LANES], kn_ref[...]).astype(BF16)
        k_out[:, h * 2 * LANES + LANES:(h + 1) * 2 * LANES] = k_rot
        vt_out[h * LANES:(h + 1) * LANES, :] = kvh[:, LANES:].T.astype(BF16)


def _mla_prep(h3, g, w, cqn, ckvn, wuq, wukv, qn, qr, qrs, kn, kr, krs, cos2, sin2, tm):
    b, s, d = h3.shape
    smalls = [g, w, cqn, ckvn, wuq, wukv, qn, qr, qrs, kn, kr, krs]
    tab = pl.BlockSpec((tm, LANES), lambda bi, i: (i, 0))
    out = lambda c: pl.BlockSpec((None, tm, c), lambda bi, i: (bi, i, 0))
    return pl.pallas_call(
        _mla_prep_kernel,
        grid=(b, s // tm),
        in_specs=[pl.BlockSpec((None, tm, d), lambda bi, i: (bi, i, 0))] + [_full(a, 2) for a in smalls] + [tab, tab],
        out_specs=[out(MLA_HEADS * 2 * LANES), out(MLA_HEADS * 2 * LANES),
                   pl.BlockSpec((None, None, MLA_HEADS * MLA_V, tm), lambda bi, i: (bi, i, 0, 0))],
        out_shape=[jax.ShapeDtypeStruct((b, s, MLA_HEADS * 2 * LANES), BF16),
                   jax.ShapeDtypeStruct((b, s, MLA_HEADS * 2 * LANES), BF16),
                   jax.ShapeDtypeStruct((b, s // tm, MLA_HEADS * MLA_V, tm), BF16)],
        compiler_params=_params("parallel", "parallel"),
        name="mla_prep",
    )(h3, *smalls, cos2, sin2)


def _diff_prep_kernel(x_ref, g_ref, w_ref, qn_ref, kn_ref, q1_out, q2_out, k_out, vt_out):
    xn = _rms_rows(x_ref[...], g_ref[...]).astype(BF16)
    u = _dot(xn, w_ref[...])
    width = DIFF_HEADS * LANES
    lane = lax.broadcasted_iota(jnp.int32, (1, LANES), 1)
    lo = lane < DIFF_QK
    scale = DIFF_QK ** -0.5 * LOG2E

    def half_rms(x, g2):
        sq = x * x
        s_lo = jnp.sum(jnp.where(lo, sq, 0.0), axis=-1, keepdims=True)
        s_hi = jnp.sum(jnp.where(lo, 0.0, sq), axis=-1, keepdims=True)
        r = jnp.where(lo, lax.rsqrt(s_lo * (1.0 / DIFF_QK) + RMS_EPS),
                      lax.rsqrt(s_hi * (1.0 / DIFF_QK) + RMS_EPS))
        return x * r * g2

    for h in range(DIFF_HEADS):
        sl = slice(h * LANES, (h + 1) * LANES)
        qn = half_rms(u[:, sl], qn_ref[...]) * scale
        q1_out[:, sl] = jnp.where(lo, qn, 0.0).astype(BF16)
        q2_out[:, sl] = jnp.where(lo, 0.0, qn).astype(BF16)
        k_out[:, sl] = half_rms(u[:, width + h * LANES:width + (h + 1) * LANES], kn_ref[...]).astype(BF16)
    vt_out[...] = u[:, 2 * width:].T.astype(BF16)


def _diff_prep(h, g, w, qn2, kn2, tm):
    m, d = h.shape
    width = DIFF_HEADS * LANES
    out = pl.BlockSpec((tm, width), lambda i: (i, 0))
    return pl.pallas_call(
        _diff_prep_kernel,
        grid=(m // tm,),
        in_specs=[pl.BlockSpec((tm, d), lambda i: (i, 0)), _full(g, 1), _full(w, 1), _full(qn2, 1), _full(kn2, 1)],
        out_specs=[out, out, out, pl.BlockSpec((None, width, tm), lambda i: (i, 0, 0))],
        out_shape=[jax.ShapeDtypeStruct((m, width), BF16)] * 3
        + [jax.ShapeDtypeStruct((m // tm, width, tm), BF16)],
        compiler_params=_params("parallel"),
        name="diff_prep",
    )(h, g, w, qn2, kn2)


def _nsa_prep_kernel(x_ref, g_ref, w_ref, qn_ref, ksn_ref, kwn_ref, q_out, k_out, raw_out, vst_out,
                     vwt_out, gt_out):
    xn = _rms_rows(x_ref[...], g_ref[...]).astype(BF16)
    u = _dot(xn, w_ref[...])
    scale = NSA_D ** -0.5 * LOG2E
    for h in range(NSA_HEADS):
        sl = slice(h * LANES, (h + 1) * LANES)
        q_out[:, sl] = (_rms_rows(u[:, sl], qn_ref[...]) * scale).astype(BF16)
    base = NSA_HEADS * LANES
    piece = lambda n: u[:, base + n * LANES:base + (n + 1) * LANES]
    raw_out[:, :LANES] = piece(0)
    raw_out[:, LANES:] = piece(1)
    k_out[:, :LANES] = _rms_rows(piece(2), ksn_ref[...]).astype(BF16)
    k_out[:, LANES:] = _rms_rows(piece(4), kwn_ref[...]).astype(BF16)
    vst_out[...] = piece(3).T.astype(BF16)
    vw_t = piece(5).T.astype(BF16)
    gate_t = jax.nn.sigmoid(piece(6)).T
    for c in range(vwt_out.shape[0]):
        vwt_out[c] = vw_t[:, c * LANES:(c + 1) * LANES]
        gt_out[c] = gate_t[:, c * LANES:(c + 1) * LANES]


def _nsa_prep(h, g, w, qn, ksn, kwn, tm):
    m, d = h.shape
    nblk = tm // LANES
    out = lambda c: pl.BlockSpec((tm, c), lambda i: (i, 0))
    blocks = pl.BlockSpec((nblk, LANES, LANES), lambda i: (i, 0, 0))
    return pl.pallas_call(
        _nsa_prep_kernel,
        grid=(m // tm,),
        in_specs=[pl.BlockSpec((tm, d), lambda i: (i, 0))] + [_full(a, 1) for a in (g, w, qn, ksn, kwn)],
        out_specs=[out(NSA_HEADS * LANES), out(2 * LANES), out(2 * LANES),
                   pl.BlockSpec((None, NSA_D, tm), lambda i: (i, 0, 0)), blocks, blocks],
        out_shape=[jax.ShapeDtypeStruct((m, NSA_HEADS * LANES), BF16),
                   jax.ShapeDtypeStruct((m, 2 * LANES), BF16),
                   jax.ShapeDtypeStruct((m, 2 * LANES), F32),
                   jax.ShapeDtypeStruct((m // tm, NSA_D, tm), BF16),
                   jax.ShapeDtypeStruct((m // LANES, LANES, LANES), BF16),
                   jax.ShapeDtypeStruct((m // LANES, LANES, LANES), F32)],
        compiler_params=_params("parallel"),
        name="nsa_prep",
    )(h, g, w, qn, ksn, kwn)


def _nsa_compress_kernel(zk_ref, zv_ref, pek_ref, pev_ref, wk_ref, wv_ref, kcn_ref, kc_out, vct_out):
    cs = NSA_CMP_STRIDE
    nchunk = zk_ref.shape[0] // cs

    def compress(z_ref, pe_ref, w_ref):
        first = jnp.zeros((nchunk, NSA_D), F32)
        second = jnp.zeros((nchunk, NSA_D), F32)
        for l in range(cs):
            z = z_ref[pl.ds(l, nchunk, stride=cs), :]
            first += _dot((z + pe_ref[l:l + 1, :]).astype(BF16), w_ref[l * NSA_D:(l + 1) * NSA_D, :])
            second += _dot((z + pe_ref[cs + l:cs + l + 1, :]).astype(BF16),
                           w_ref[(cs + l) * NSA_D:(cs + l + 1) * NSA_D, :])
        return first + pltpu.roll(second, shift=nchunk - 1, axis=0)

    kc_out[...] = _rms_rows(compress(zk_ref, pek_ref, wk_ref), kcn_ref[...]).astype(BF16)
    vct_out[...] = compress(zv_ref, pev_ref, wv_ref).T.astype(BF16)


def _nsa_compress(raw, pek, pev, wk, wv, kcn):
    b, s, _ = raw.shape
    nchunk = s // NSA_CMP_STRIDE
    return pl.pallas_call(
        _nsa_compress_kernel,
        grid=(b,),
        in_specs=[pl.BlockSpec((None, s, NSA_D), lambda bi: (bi, 0, 0)),
                  pl.BlockSpec((None, s, NSA_D), lambda bi: (bi, 0, 1))]
        + [_full(a, 1) for a in (pek, pev, wk, wv, kcn)],
        out_specs=[pl.BlockSpec((None, nchunk, NSA_D), lambda bi: (bi, 0, 0)),
                   pl.BlockSpec((None, NSA_D, nchunk), lambda bi: (bi, 0, 0))],
        out_shape=[jax.ShapeDtypeStruct((b, nchunk, NSA_D), BF16),
                   jax.ShapeDtypeStruct((b, NSA_D, nchunk), BF16)],
        compiler_params=_params("parallel"),
        name="nsa_compress",
    )(raw, raw, pek, pev, wk, wv, kcn)


def _softmax_step(s, vt, m, l, acc):
    m_new = jnp.maximum(m, jnp.max(s, axis=0, keepdims=True))
    alpha = jnp.exp2(m - m_new)
    p = jnp.exp2(s - m_new)
    l = alpha * l + jnp.sum(p, axis=0, keepdims=True)
    acc = alpha * acc + _dot(vt, p.astype(BF16))
    return m_new, l, acc


def _softmax_init(nq, dv):
    return (jnp.full((1, nq), NEG_INF, F32), jnp.zeros((1, nq), F32), jnp.zeros((dv, nq), F32))


def _softmax_out(l, acc):
    return acc * (1.0 / jnp.maximum(l, 1e-30))


def _softmax_exact(s, mask):
    m = jnp.max(s, axis=0, keepdims=True)
    e = jnp.where(mask, jnp.exp2(s - m), 0.0)
    return e * (1.0 / jnp.maximum(jnp.sum(e, axis=0, keepdims=True), 1e-30))


def _causal_mask(tk, tq, d, strict=False, reps=1):
    key = d * tk + lax.broadcasted_iota(jnp.int32, (tk, reps * tq), 0)
    qry = lax.broadcasted_iota(jnp.int32, (tk, reps * tq), 1)
    for _ in range(1, reps):
        qry = jnp.where(qry >= tq, qry - tq, qry)
    return key < qry if strict else key <= qry


def _lanes(parts):
    return jnp.concatenate(parts, axis=1)


def _mla_attn_kernel(q_ref, k_ref, vt_ref, on_ref, o_ref, *, tq, tk):
    i = pl.program_id(1)
    r = tq // tk
    dq = 2 * LANES
    qs = [q_ref[:, h * dq:(h + 1) * dq] for h in range(MLA_HEADS)]

    def step(j, carry, mask):
        st = pl.multiple_of(j * tk, tk)
        scores = lambda h: _dot_nt(k_ref[pl.ds(st, tk), h * dq:(h + 1) * dq], qs[h])
        out = ()
        s_next = scores(0)
        for h in range(MLA_HEADS):
            s, s_next = s_next, (scores(h + 1) if h + 1 < MLA_HEADS else None)
            if mask is not None:
                s = jnp.where(mask, s, NEG_INF)
            out += _softmax_step(s, vt_ref[j, h * MLA_V:(h + 1) * MLA_V, :], *carry[3 * h:3 * h + 3])
        return out

    c = lax.fori_loop(0, i * r, lambda j, c: step(j, c, None), _softmax_init(tq, MLA_V) * MLA_HEADS)
    for d in range(r):
        c = step(i * r + d, c, _causal_mask(tk, tq, d))
    for h in range(MLA_HEADS):
        o = _softmax_out(c[3 * h + 1], c[3 * h + 2])
        o_ref[:, h * MLA_V:(h + 1) * MLA_V] = _rms_cols(o, on_ref[...]).T.astype(o_ref.dtype)


def _mla_attn(q, k, vt, on, tq, tk):
    b, s, _ = q.shape
    return pl.pallas_call(
        functools.partial(_mla_attn_kernel, tq=tq, tk=tk),
        grid=(b, s // tq),
        in_specs=[pl.BlockSpec((None, tq, q.shape[2]), lambda bi, i: (bi, i, 0)),
                  pl.BlockSpec((None, s, k.shape[2]), lambda bi, i: (bi, 0, 0)),
                  pl.BlockSpec((None,) + vt.shape[1:], lambda bi, i: (bi, 0, 0, 0)),
                  _full(on, 2)],
        out_specs=pl.BlockSpec((None, tq, MLA_HEADS * MLA_V), lambda bi, i: (bi, i, 0)),
        out_shape=jax.ShapeDtypeStruct((b, s, MLA_HEADS * MLA_V), BF16),
        compiler_params=_params("parallel", "arbitrary"),
        name="mla_attn",
    )(q, k, vt, on)


def _diff_attn_kernel(q1_ref, q2_ref, k_ref, vt_ref, lq1_ref, lk1_ref, lq2_ref, lk2_ref,
                      sub_ref, o_ref, *, tq, tk, lambda_init):
    i = pl.program_id(1)
    r = tq // tk
    t0 = i * tq
    keyrel = lax.broadcasted_iota(jnp.int32, (tk, 1), 0)
    qs = [jnp.concatenate([q1_ref[:, h * LANES:(h + 1) * LANES], q2_ref[:, h * LANES:(h + 1) * LANES]], axis=0)
          for h in range(DIFF_HEADS)]

    def step(j, carry, mask):
        st = pl.multiple_of(j * tk, tk)
        rel = (keyrel + (j * tk - t0)).astype(F32)
        scores = lambda h: _dot_nt(k_ref[pl.ds(st, tk), h * LANES:(h + 1) * LANES], qs[h])
        out = ()
        s_next = scores(0)
        for h in range(DIFF_HEADS):
            sl = slice(h * LANES, (h + 1) * LANES)
            s, s_next = s_next, (scores(h + 1) if h + 1 < DIFF_HEADS else None)
            s = s + DIFF_SLOPES[h] * rel
            if mask is not None:
                s = jnp.where(mask, s, NEG_INF)
            out += _softmax_step(s, vt_ref[j, sl, :], *carry[3 * h:3 * h + 3])
        return out

    c = lax.fori_loop(0, i * r, lambda j, c: step(j, c, None), _softmax_init(2 * tq, DIFF_V) * DIFF_HEADS)
    for d in range(r):
        c = step(i * r + d, c, _causal_mask(tk, tq, d, reps=2))
    lam = (jnp.exp(jnp.sum(lq1_ref[...] * lk1_ref[...], axis=-1, keepdims=True))
           - jnp.exp(jnp.sum(lq2_ref[...] * lk2_ref[...], axis=-1, keepdims=True)) + lambda_init)
    for h in range(DIFF_HEADS):
        o12 = _softmax_out(c[3 * h + 1], c[3 * h + 2])
        o = o12[:, :tq] - lam * o12[:, tq:]
        o_ref[:, h * DIFF_V:(h + 1) * DIFF_V] = (
            _rms_cols(o, sub_ref[...]) * (1.0 - lambda_init)).T.astype(o_ref.dtype)


def _diff_attn(q1, q2, k, vt, lq1, lk1, lq2, lk2, sub, lambda_init, tq, tk):
    b, s, w = q1.shape
    qs = pl.BlockSpec((None, tq, w), lambda bi, i: (bi, i, 0))
    smalls = [lq1, lk1, lq2, lk2, sub]
    return pl.pallas_call(
        functools.partial(_diff_attn_kernel, tq=tq, tk=tk, lambda_init=lambda_init),
        grid=(b, s // tq),
        in_specs=[qs, qs, pl.BlockSpec((None, s, w), lambda bi, i: (bi, 0, 0)),
                  pl.BlockSpec((None,) + vt.shape[1:], lambda bi, i: (bi, 0, 0, 0))]
        + [_full(a, 2) for a in smalls],
        out_specs=pl.BlockSpec((None, tq, DIFF_HEADS * DIFF_V), lambda bi, i: (bi, i, 0)),
        out_shape=jax.ShapeDtypeStruct((b, s, DIFF_HEADS * DIFF_V), BF16),
        compiler_params=_params("parallel", "arbitrary"),
        name="diff_attn",
    )(q1, q2, k, vt, *smalls)


def _sb_attn_kernel(q_ref, k_ref, vt_ref, tri_ref, on_ref, o_ref, *, tq, tk):
    i = pl.program_id(1)
    r = tq // tk
    tri = tri_ref[...]
    tb = tri.shape[0]
    qs = [q_ref[:, h * SB_D:(h + 1) * SB_D] for h in range(SB_HEADS)]

    def step(j, carry, mask):
        st = pl.multiple_of(j * tk, tk)
        logits = lambda h: _dot_nt(k_ref[pl.ds(st, tk), h * SB_D:(h + 1) * SB_D], qs[h])
        out = ()
        z_next = logits(0)
        for h in range(SB_HEADS):
            run, acc = carry[2 * h:2 * h + 2]
            sl = slice(h * SB_D, (h + 1) * SB_D)
            z, z_next = z_next, (logits(h + 1) if h + 1 < SB_HEADS else None)
            log_sig = jnp.minimum(z, 0.0) - jnp.log2(1.0 + jnp.exp2(-jnp.abs(z)))
            log_not = log_sig - z
            if mask is not None:
                log_not = jnp.where(mask, log_not, 0.0)
            hi = log_not.astype(BF16)
            lo = (log_not - hi.astype(F32)).astype(BF16)
            after = []
            for blk in reversed(range(tk // tb)):
                rows = slice(blk * tb, (blk + 1) * tb)
                after.insert(0, _dot(tri, hi[rows]) + _dot(tri, lo[rows]) + run)
                run = run + jnp.sum(log_not[rows], axis=0, keepdims=True)
            a = jnp.exp2(log_sig + jnp.concatenate(after, axis=0))
            if mask is not None:
                a = jnp.where(mask, a, 0.0)
            acc = acc + _dot(vt_ref[j, sl, :], a.astype(BF16))
            out += (run, acc)
        return out

    carry = (jnp.zeros((1, tq), F32), jnp.zeros((SB_D, tq), F32)) * SB_HEADS
    for d in reversed(range(r)):
        carry = step(i * r + d, carry, _causal_mask(tk, tq, d, strict=True))
    carry = lax.fori_loop(0, i * r, lambda n, c: step(i * r - 1 - n, c, None), carry)
    for h in range(SB_HEADS):
        o_ref[:, h * SB_D:(h + 1) * SB_D] = _rms_cols(carry[2 * h + 1], on_ref[...]).T.astype(o_ref.dtype)


def _sb_attn(qk, vt, tri, on, tq, tk):
    b, s, _ = qk.shape
    w = SB_HEADS * SB_D
    return pl.pallas_call(
        functools.partial(_sb_attn_kernel, tq=tq, tk=tk),
        grid=(b, s // tq),
        in_specs=[pl.BlockSpec((None, tq, w), lambda bi, i: (bi, i, 0)),
                  pl.BlockSpec((None, s, w), lambda bi, i: (bi, 0, 1)),
                  pl.BlockSpec((None,) + vt.shape[1:], lambda bi, i: (bi, 0, 0, 0)),
                  _full(tri, 2), _full(on, 2)],
        out_specs=pl.BlockSpec((None, tq, w), lambda bi, i: (bi, i, 0)),
        out_shape=jax.ShapeDtypeStruct((b, s, w), BF16),
        compiler_params=_params("parallel", "arbitrary"),
        name="sb_attn",
    )(qk, qk, vt, tri, on)


def _topk_mask(score, n):
    blk = lax.broadcasted_iota(jnp.int32, score.shape, 0).astype(F32)

    def body(_, c):
        work, sel = c
        mx = jnp.max(work, axis=0, keepdims=True)
        idx = jnp.min(jnp.where(work == mx, blk, 1e9), axis=0, keepdims=True)
        pick = blk == idx
        return jnp.where(pick, -jnp.inf, work), jnp.where(pick, 1.0, sel)

    _, sel = lax.fori_loop(0, n, body, (score, jnp.zeros_like(score)), unroll=True)
    return sel


def _nsa_attn_kernel(q_ref, gt_ref, kc_ref, vct_ref, ks_ref, kw_ref, vst_ref, vwt_ref, ovt_ref, ext_ref,
                     on_ref, o_ref, selb_ref, *, t, tk, n_sb, n_sel):
    i = pl.program_id(1)
    t0 = i * t
    ncp = kc_ref.shape[0]
    nh = NSA_HEADS
    q4 = jnp.concatenate([q_ref[:, h * LANES:(h + 1) * LANES] for h in range(nh)], axis=0)
    qpos1 = t0 + lax.broadcasted_iota(jnp.int32, (1, t), 1)
    qpos4 = _lanes([qpos1] * nh)
    slope4 = _lanes([jnp.full((1, t), NSA_SLOPES[h], F32) for h in range(nh)])

    c_end = NSA_CMP_STRIDE * lax.broadcasted_iota(jnp.int32, (ncp, 1), 0) + (NSA_CMP_LEN - 1)
    cmask = c_end <= qpos4
    s = jnp.where(cmask, _dot_nt(kc_ref[...], q4) + slope4 * (c_end - t0).astype(F32), NEG_INF)
    p = _softmax_exact(s, cmask)
    o_cmp = _dot(vct_ref[...], p.astype(BF16))
    pc_sum = p[:, 0:t]
    for h in range(1, nh):
        pc_sum = pc_sum + p[:, h * t:(h + 1) * t]
    ovt = ovt_ref[...]
    hi = pc_sum.astype(BF16)
    r1 = pc_sum - hi.astype(F32)
    mid = r1.astype(BF16)
    lo = (r1 - mid.astype(F32)).astype(BF16)
    imp = _dot(ovt, hi) + _dot(ovt, mid) + _dot(ovt, lo)

    jb = lax.broadcasted_iota(jnp.int32, (ovt.shape[0], 1), 0)
    cur = qpos1 >> int(math.log2(NSA_SEL_BLOCK))
    valid = (jb * NSA_SEL_BLOCK <= qpos1) & (jb < n_sb)
    forced = (jb == 0) | (jb == cur) | (jb == cur - 1)
    score = jnp.where(valid, imp + jnp.where(forced, NSA_FORCE_BONUS, 0.0), NEG_INF)
    unselected = (jnp.where(score > 0.5 * NEG_INF, _topk_mask(score, n_sel), 0.0) - 1.0).astype(BF16)

    nwin = NSA_WINDOW // t + 1
    wblk = jnp.maximum(i - NSA_WINDOW // t, 0)
    w0 = pl.multiple_of(wblk * t, t)
    wcol = w0 + lax.broadcasted_iota(jnp.int32, (nwin * t, 1), 0)
    dist = qpos4 - wcol
    wmask = (dist >= 0) & (dist < NSA_WINDOW)
    s = _dot_nt(kw_ref[pl.ds(w0, nwin * t), :], q4) + slope4 * (wcol - t0).astype(F32)
    s = jnp.where(wmask, s, NEG_INF)
    e = jnp.exp2(s - jnp.max(s, axis=0, keepdims=True))
    p = (e * (1.0 / jnp.maximum(jnp.sum(e, axis=0, keepdims=True), 1e-30))).astype(BF16)
    o_win = _dot(vwt_ref[wblk], p[0:t])
    for c in range(1, nwin):
        o_win = o_win + _dot(vwt_ref[wblk + c], p[c * t:(c + 1) * t])

    for jj in range(selb_ref.shape[0]):
        selb_ref[jj] = _dot(ext_ref[jj * tk:(jj + 1) * tk, :], unselected)

    def sel_step(j, c, causal):
        st = pl.multiple_of(j * tk, tk)
        col = st + lax.broadcasted_iota(jnp.int32, (tk, 1), 0)
        s = (_dot_nt(ks_ref[pl.ds(st, tk), :], q4) + slope4 * (col - t0).astype(F32)
             + _lanes([selb_ref[j]] * nh))
        if causal:
            s = jnp.where(col <= qpos4, s, NEG_INF)
        return _softmax_step(s, vst_ref[j], *c)

    n_full = t0 // tk
    c = lax.fori_loop(0, n_full, lambda j, c: sel_step(j, c, False), _softmax_init(nh * t, NSA_D))
    _, l, acc = sel_step(n_full, c, True)
    o_sel = _softmax_out(l, acc)

    gates = gt_ref[...]
    for h in range(nh):
        cols = slice(h * t, (h + 1) * t)
        o = (gates[3 * h:3 * h + 1] * o_cmp[:, cols] + gates[3 * h + 1:3 * h + 2] * o_sel[:, cols]
             + gates[3 * h + 2:3 * h + 3] * o_win[:, cols])
        o_ref[:, h * LANES:(h + 1) * LANES] = _rms_cols(o, on_ref[...]).T.astype(o_ref.dtype)


def _nsa_attn(q, gt, kc, vct, kk, vst, vwt, ovt, ext, on, t, tk):
    b, s, _ = q.shape
    ncp = kc.shape[1]
    n_sb = s // NSA_SEL_BLOCK
    n_sel = min(NSA_N_SELECT, n_sb)
    k_spec = lambda n: pl.BlockSpec((None, s, NSA_D), lambda bi, i: (bi, 0, n))
    whole = lambda a: pl.BlockSpec((None,) + a.shape[1:], lambda bi, i: (bi,) + (0,) * (a.ndim - 1))
    return pl.pallas_call(
        functools.partial(_nsa_attn_kernel, t=t, tk=tk, n_sb=n_sb, n_sel=n_sel),
        grid=(b, s // t),
        in_specs=[pl.BlockSpec((None, t, NSA_HEADS * NSA_D), lambda bi, i: (bi, i, 0)),
                  pl.BlockSpec((None, None, LANES, t), lambda bi, i: (bi, i, 0, 0)),
                  whole(kc), whole(vct), k_spec(0), k_spec(1), whole(vst), whole(vwt),
                  _full(ovt, 2), _full(ext, 2), _full(on, 2)],
        out_specs=pl.BlockSpec((None, t, NSA_HEADS * NSA_D), lambda bi, i: (bi, i, 0)),
        out_shape=jax.ShapeDtypeStruct((b, s, NSA_HEADS * NSA_D), BF16),
        scratch_shapes=[pltpu.VMEM((s // tk, tk, t), F32)],
        compiler_params=_params("parallel", "arbitrary"),
        name="nsa_attn",
    )(q, gt, kc, vct, kk, kk, vst, vwt, ovt, ext, on)


def _pad_cols(w, width):
    return jnp.pad(w, ((0, 0), (0, width - w.shape[1])))


def _swap_halves(w):
    half = w.shape[-1] // 2
    return jnp.concatenate([w[..., half:], w[..., :half]], axis=-1)


def _row(v):
    return v.reshape(1, -1).astype(F32)


def _col(v):
    return v.reshape(-1, 1).astype(F32)


def _nsa_constants(s):
    ncp = s // NSA_CMP_STRIDE
    n_sb = s // NSA_SEL_BLOCK
    nb = -(-n_sb // 16) * 16
    c_start = NSA_CMP_STRIDE * np.arange(ncp)[None, :]
    sel_start = NSA_SEL_BLOCK * np.arange(nb)[:, None]
    overlap_t = ((c_start < sel_start + NSA_SEL_BLOCK) & (c_start + NSA_CMP_LEN > sel_start)
                 & (np.arange(nb)[:, None] < n_sb))
    expand_t = (np.arange(s)[:, None] // NSA_SEL_BLOCK) == np.arange(nb)[None, :]
    return jnp.asarray(overlap_t, BF16), jnp.asarray(expand_t * MASK_BIG, BF16)


def kernel(x, ffn1_norm, ffn1_w_gate, ffn1_w_up, ffn1_w_down, mix_norm, w_in, mla_cq_norm, mla_ckv_norm, mla_w_uq, mla_w_ukv, mla_qn_norm, mla_qr_norm, mla_kn_norm, mla_kr_norm, mla_o_norm, diff_q_norm, diff_k_norm, diff_lq1, diff_lk1, diff_lq2, diff_lk2, diff_subln, nsa_q_norm, nsa_pe_k, nsa_w_ck, nsa_pe_v, nsa_w_cv, nsa_kc_norm, nsa_ks_norm, nsa_kw_norm, nsa_o_norm, sb_o_norm, w_out, ffn2_norm, ffn2_w_gate, ffn2_w_up, ffn2_w_down):
    b, s, d = x.shape
    m = b * s
    depth = w_in.shape[0]

    pos = np.arange(s, dtype=np.float64)
    inv_freq = ROPE_THETA ** (-np.arange(0, MLA_ROPE, 2, dtype=np.float64) / MLA_ROPE)
    ang = pos[:, None] * inv_freq[None, :]
    cos, sin = np.cos(ang), np.sin(ang)
    zeros = np.zeros((s, LANES - MLA_ROPE))
    cos2 = jnp.asarray(np.concatenate([cos, cos, zeros], axis=-1), F32)
    sin2 = jnp.asarray(np.concatenate([-sin, sin, zeros], axis=-1), F32)
    tri = jnp.asarray(np.arange(SB_TRI)[None, :] > np.arange(SB_TRI)[:, None], BF16)
    overlap_t, expand_t = _nsa_constants(s)

    h = x.reshape(m, d)
    for l in range(depth):
        lambda_init = 0.8 - 0.6 * math.exp(-0.3 * l)
        h = _ffn(h, ffn1_norm[l], ffn1_w_gate[l].astype(BF16), ffn1_w_up[l].astype(BF16),
                 ffn1_w_down[l].astype(BF16))

        g_mix = _row(mix_norm[l])
        wl = w_in[l]
        o0 = 0
        w_mla, o0 = wl[:, o0:o0 + W_MLA], o0 + W_MLA
        w_diff, o0 = wl[:, o0:o0 + W_DIFF], o0 + W_DIFF
        w_nsa, o0 = wl[:, o0:o0 + W_NSA], o0 + W_NSA
        w_sb = wl[:, o0:o0 + W_SB]
        per_batch = lambda a: a.reshape((b, a.shape[0] // b) + a.shape[1:])

        w_kr = w_mla[:, MLA_Q_RANK + MLA_KV_RANK:]
        w_mla_p = jnp.concatenate([w_mla[:, :MLA_Q_RANK + MLA_KV_RANK], _pad_cols(w_kr, LANES),
                                   _pad_cols(_swap_halves(w_kr), LANES)], axis=1).astype(BF16)
        wuq = mla_w_uq[l].reshape(MLA_Q_RANK, MLA_HEADS, MLA_NOPE + MLA_ROPE)
        wuq_r = wuq[..., MLA_NOPE:]
        pad3 = lambda a: jnp.pad(a, ((0, 0), (0, 0), (0, LANES - a.shape[-1])))
        wuq_p = jnp.concatenate([wuq[..., :MLA_NOPE], pad3(wuq_r), pad3(_swap_halves(wuq_r))],
                                axis=-1).reshape(MLA_Q_RANK, MLA_HEADS * 3 * LANES).astype(BF16)
        padr = lambda v: _pad_cols(_row(v), LANES)
        q_mla, k_mla, vt_mla = _mla_prep(
            h.reshape(b, s, d), g_mix, w_mla_p, _row(mla_cq_norm[l]), _row(mla_ckv_norm[l]), wuq_p,
            mla_w_ukv[l].astype(BF16), _row(mla_qn_norm[l]), padr(mla_qr_norm[l]),
            padr(_swap_halves(mla_qr_norm[l])), _row(mla_kn_norm[l]), padr(mla_kr_norm[l]),
            padr(_swap_halves(mla_kr_norm[l])), cos2, sin2, tm=MLA_TK)
        o_a = _mla_attn(q_mla, k_mla, vt_mla, _col(mla_o_norm[l]), MLA_TQ, MLA_TK)

        two = lambda v: jnp.concatenate([_row(v), _row(v)], axis=1)
        q1, q2, k_d, vt_d = _diff_prep(h, g_mix, w_diff.astype(BF16), two(diff_q_norm[l]),
                                       two(diff_k_norm[l]), tm=DIFF_TK)
        o_b = _diff_attn(per_batch(q1), per_batch(q2), per_batch(k_d), per_batch(vt_d), _row(diff_lq1[l]),
                         _row(diff_lk1[l]), _row(diff_lq2[l]), _row(diff_lk2[l]), _col(diff_subln[l]),
                         lambda_init, DIFF_TQ, DIFF_TK)

        w_nsa_p = _pad_cols(w_nsa, NSA_HEADS * LANES + 7 * LANES).astype(BF16)
        q_n, k_n, raw_n, vst_n, vwt_n, gt_n = _nsa_prep(
            h, g_mix, w_nsa_p, _row(nsa_q_norm[l]), _row(nsa_ks_norm[l]), _row(nsa_kw_norm[l]), tm=NSA_TK)
        kc, vct = _nsa_compress(per_batch(raw_n), nsa_pe_k[l], nsa_pe_v[l], nsa_w_ck[l].astype(BF16),
                                nsa_w_cv[l].astype(BF16), _row(nsa_kc_norm[l]))
        o_c = _nsa_attn(per_batch(q_n), per_batch(gt_n), kc, vct, per_batch(k_n), per_batch(vst_n),
                        per_batch(vwt_n), overlap_t, expand_t, _col(nsa_o_norm[l]), NSA_TQ, NSA_TK)

        qk_sb, vt_sb = _sb_prep(h, g_mix, w_sb.astype(BF16), tm=SB_TK)
        o_d = _sb_attn(per_batch(qk_sb), per_batch(vt_sb), tri, _col(sb_o_norm[l]), SB_TQ, SB_TK)

        flat = lambda a: a.reshape(m, -1)
        h = _out_proj([flat(o_a), flat(o_b), flat(o_c), flat(o_d)], w_out[l].astype(BF16), h)
        h = _ffn(h, ffn2_norm[l], ffn2_w_gate[l].astype(BF16), ffn2_w_up[l].astype(BF16),
                 ffn2_w_down[l].astype(BF16))
    return h.reshape(b, s, d)
```

```python
import functools
import math

import numpy as np
import jax
import jax.numpy as jnp
from jax import lax
from jax.experimental import pallas as pl
from jax.experimental.pallas import tpu as pltpu

F32 = jnp.float32
BF16 = jnp.bfloat16

RMS_EPS = 1e-6
NEG_INF = -1e30
MASK_BIG = 2.0 ** 100
LANES = 128
VMEM_LIMIT = 56 * 1024 * 1024

MLA_HEADS, MLA_Q_RANK, MLA_KV_RANK, MLA_NOPE, MLA_ROPE, MLA_V = 4, 512, 256, 128, 64, 128
ROPE_THETA = 10000.0
DIFF_HEADS, DIFF_QK, DIFF_V = 4, 64, 128
NSA_HEADS, NSA_D = 4, 128
NSA_CMP_LEN, NSA_CMP_STRIDE, NSA_SEL_BLOCK, NSA_N_SELECT, NSA_WINDOW = 32, 16, 64, 16, 512
NSA_FORCE_BONUS = 1e4
SB_HEADS, SB_D = 4, 128
N_ALIBI_HEADS = DIFF_HEADS + NSA_HEADS
ALIBI_SLOPES = [2.0 ** (-8.0 * k / N_ALIBI_HEADS) for k in range(1, N_ALIBI_HEADS + 1)]
LOG2E = math.log2(math.e)
DIFF_SLOPES = [sl * LOG2E for sl in ALIBI_SLOPES[0::2]]
NSA_SLOPES = [sl * LOG2E for sl in ALIBI_SLOPES[1::2]]

W_MLA = MLA_Q_RANK + MLA_KV_RANK + MLA_ROPE
W_DIFF = DIFF_HEADS * (2 * DIFF_QK + 2 * DIFF_QK + DIFF_V)
W_NSA = NSA_HEADS * NSA_D + 6 * NSA_D + NSA_HEADS * 3
W_SB = 3 * SB_HEADS * SB_D

MLA_TQ, MLA_TK = 512, 512
DIFF_TQ, DIFF_TK = 512, 512
SB_TQ, SB_TK = 512, 512
SB_TRI = 256
NSA_TQ, NSA_TK = 256, 512


def _dot(a, b):
    return jnp.dot(a, b, preferred_element_type=F32)


def _dot_nt(a, b):
    return lax.dot_general(a, b, (((1,), (1,)), ((), ())), preferred_element_type=F32)


def _rms_rows(x, g):
    ms = jnp.mean(x * x, axis=-1, keepdims=True)
    return x * lax.rsqrt(ms + RMS_EPS) * g


def _rms_cols(x, g):
    ms = jnp.mean(x * x, axis=0, keepdims=True)
    return x * lax.rsqrt(ms + RMS_EPS) * g


def _params(*sem):
    return pltpu.CompilerParams(dimension_semantics=sem, vmem_limit_bytes=VMEM_LIMIT)


def _full(a, grid_rank):
    return pl.BlockSpec(a.shape, lambda *_: (0,) * a.ndim)


def _ffn_kernel(x_ref, g_ref, wg_ref, wu_ref, wd_ref, o_ref, xn_ref, acc_ref, *, nf):
    f = pl.program_id(1)

    @pl.when(f == 0)
    def _():
        xn_ref[...] = _rms_rows(x_ref[...], g_ref[...]).astype(BF16)
        acc_ref[...] = jnp.zeros_like(acc_ref)

    xn = xn_ref[...]
    gate = _dot(xn, wg_ref[...])
    up = _dot(xn, wu_ref[...])
    mid = (gate * jax.nn.sigmoid(gate) * up).astype(BF16)
    acc_ref[...] += _dot(mid, wd_ref[...])

    @pl.when(f == nf - 1)
    def _():
        o_ref[...] = x_ref[...] + 0.5 * acc_ref[...]


def _ffn(h, g, wg, wu, wd, tm=512, tf=512):
    m, d = h.shape
    f = wg.shape[1]
    return pl.pallas_call(
        functools.partial(_ffn_kernel, nf=f // tf),
        grid=(m // tm, f // tf),
        in_specs=[
            pl.BlockSpec((tm, d), lambda i, j: (i, 0)),
            pl.BlockSpec((1, d), lambda i, j: (0, 0)),
            pl.BlockSpec((d, tf), lambda i, j: (0, j)),
            pl.BlockSpec((d, tf), lambda i, j: (0, j)),
            pl.BlockSpec((tf, d), lambda i, j: (j, 0)),
        ],
        out_specs=pl.BlockSpec((tm, d), lambda i, j: (i, 0)),
        out_shape=jax.ShapeDtypeStruct((m, d), F32),
        scratch_shapes=[pltpu.VMEM((tm, d), BF16), pltpu.VMEM((tm, d), F32)],
        compiler_params=_params("parallel", "arbitrary"),
        name="ffn",
    )(h, g.reshape(1, d), wg, wu, wd)


def _out_proj_kernel(a0_ref, a1_ref, a2_ref, a3_ref, w_ref, r_ref, o_ref):
    acc = r_ref[...]
    for n, a_ref in enumerate((a0_ref, a1_ref, a2_ref, a3_ref)):
        kw = a_ref.shape[1]
        acc = acc + _dot(a_ref[...], w_ref[n * kw:(n + 1) * kw, :])
    o_ref[...] = acc


def _out_proj(parts, w, res, tm=512, tn=2048):
    m, d = res.shape
    kw = parts[0].shape[1]
    a_spec = pl.BlockSpec((tm, kw), lambda i, j: (i, 0))
    return pl.pallas_call(
        _out_proj_kernel,
        grid=(m // tm, d // tn),
        in_specs=[a_spec, a_spec, a_spec, a_spec,
                  pl.BlockSpec((w.shape[0], tn), lambda i, j: (0, j)),
                  pl.BlockSpec((tm, tn), lambda i, j: (i, j))],
        out_specs=pl.BlockSpec((tm, tn), lambda i, j: (i, j)),
        out_shape=jax.ShapeDtypeStruct((m, d), F32),
        compiler_params=_params("parallel", "arbitrary"),
        name="out_proj",
    )(*parts, w, res)


def _sb_prep_kernel(x_ref, g_ref, w_ref, qk_out, vt_out):
    xn = _rms_rows(x_ref[...], g_ref[...]).astype(BF16)
    u = _dot(xn, w_ref[...])
    width = SB_HEADS * SB_D
    qk_out[:, :width] = (u[:, :width] * (SB_D ** -0.5 * LOG2E)).astype(BF16)
    qk_out[:, width:] = u[:, width:2 * width].astype(BF16)
    vt_out[...] = u[:, 2 * width:].T.astype(BF16)


def _sb_prep(h, g, w, tm):
    m, d = h.shape
    width = SB_HEADS * SB_D
    return pl.pallas_call(
        _sb_prep_kernel,
        grid=(m // tm,),
        in_specs=[pl.BlockSpec((tm, d), lambda i: (i, 0)), _full(g, 1), _full(w, 1)],
        out_specs=[pl.BlockSpec((tm, 2 * width), lambda i: (i, 0)),
                   pl.BlockSpec((None, width, tm), lambda i: (i, 0, 0))],
        out_shape=[jax.ShapeDtypeStruct((m, 2 * width), BF16),
                   jax.ShapeDtypeStruct((m // tm, width, tm), BF16)],
        compiler_params=_params("parallel"),
        name="sb_prep",
    )(h, g, w)


def _rope_rows(x, x_sw, g, g_sw, cos2, sin2):
    ms = jnp.sum(x * x, axis=-1, keepdims=True) * (1.0 / MLA_ROPE)
    r = lax.rsqrt(ms + RMS_EPS)
    return (x * r * g) * cos2 + (x_sw * r * g_sw) * sin2


def _mla_prep_kernel(x_ref, g_ref, w_ref, cqn_ref, ckvn_ref, wuq_ref, wukv_ref, qn_ref, qr_ref,
                     qrs_ref, kn_ref, kr_ref, krs_ref, cos_ref, sin_ref, q_out, k_out, vt_out):
    xn = _rms_rows(x_ref[...], g_ref[...]).astype(BF16)
    u = _dot(xn, w_ref[...])
    c_q = u[:, :MLA_Q_RANK]
    c_kv = u[:, MLA_Q_RANK:MLA_Q_RANK + MLA_KV_RANK]
    kr = u[:, MLA_Q_RANK + MLA_KV_RANK:MLA_Q_RANK + MLA_KV_RANK + LANES]
    kr_sw = u[:, MLA_Q_RANK + MLA_KV_RANK + LANES:]
    cos2, sin2 = cos_ref[...], sin_ref[...]
    qall = _dot(_rms_rows(c_q, cqn_ref[...]).astype(BF16), wuq_ref[...])
    kvall = _dot(_rms_rows(c_kv, ckvn_ref[...]).astype(BF16), wukv_ref[...])
    k_rot = _rope_rows(kr, kr_sw, kr_ref[...], krs_ref[...], cos2, sin2).astype(BF16)
    scale = (MLA_NOPE + MLA_ROPE) ** -0.5 * LOG2E
    for h in range(MLA_HEADS):
        qh = qall[:, h * 3 * LANES:(h + 1) * 3 * LANES]
        q_nope = _rms_rows(qh[:, :LANES], qn_ref[...]) * scale
        q_rot = _rope_rows(qh[:, LANES:2 * LANES], qh[:, 2 * LANES:], qr_ref[...], qrs_ref[...],
                           cos2, sin2) * scale
        q_out[:, h * 2 * LANES:h * 2 * LANES + LANES] = q_nope.astype(BF16)
        q_out[:, h * 2 * LANES + LANES:(h + 1) * 2 * LANES] = q_rot.astype(BF16)
        kvh = kvall[:, h * 2 * LANES:(h + 1) * 2 * LANES]
        k_out[:, h * 2 * LANES:h * 2 * LANES + LANES] = _rms_rows(kvh[:, :LANES], kn_ref[...]).astype(BF16)
        k_out[:, h * 2 * LANES + LANES:(h + 1) * 2 * LANES] = k_rot
        vt_out[h * LANES:(h + 1) * LANES, :] = kvh[:, LANES:].T.astype(BF16)


def _mla_prep(h3, g, w, cqn, ckvn, wuq, wukv, qn, qr, qrs, kn, kr, krs, cos2, sin2, tm):
    b, s, d = h3.shape
    smalls = [g, w, cqn, ckvn, wuq, wukv, qn, qr, qrs, kn, kr, krs]
    tab = pl.BlockSpec((tm, LANES), lambda bi, i: (i, 0))
    out = lambda c: pl.BlockSpec((None, tm, c), lambda bi, i: (bi, i, 0))
    return pl.pallas_call(
        _mla_prep_kernel,
        grid=(b, s // tm),
        in_specs=[pl.BlockSpec((None, tm, d), lambda bi, i: (bi, i, 0))] + [_full(a, 2) for a in smalls] + [tab, tab],
        out_specs=[out(MLA_HEADS * 2 * LANES), out(MLA_HEADS * 2 * LANES),
                   pl.BlockSpec((None, None, MLA_HEADS * MLA_V, tm), lambda bi, i: (bi, i, 0, 0))],
        out_shape=[jax.ShapeDtypeStruct((b, s, MLA_HEADS * 2 * LANES), BF16),
                   jax.ShapeDtypeStruct((b, s, MLA_HEADS * 2 * LANES), BF16),
                   jax.ShapeDtypeStruct((b, s // tm, MLA_HEADS * MLA_V, tm), BF16)],
        compiler_params=_params("parallel", "parallel"),
        name="mla_prep",
    )(h3, *smalls, cos2, sin2)


def _diff_prep_kernel(x_ref, g_ref, w_ref, qn_ref, kn_ref, q1_out, q2_out, k_out, vt_out):
    xn = _rms_rows(x_ref[...], g_ref[...]).astype(BF16)
    u = _dot(xn, w_ref[...])
    width = DIFF_HEADS * LANES
    lane = lax.broadcasted_iota(jnp.int32, (1, LANES), 1)
    lo = lane < DIFF_QK
    scale = DIFF_QK ** -0.5 * LOG2E

    def half_rms(x, g2):
        sq = x * x
        s_lo = jnp.sum(jnp.where(lo, sq, 0.0), axis=-1, keepdims=True)
        s_hi = jnp.sum(jnp.where(lo, 0.0, sq), axis=-1, keepdims=True)
        r = jnp.where(lo, lax.rsqrt(s_lo * (1.0 / DIFF_QK) + RMS_EPS),
                      lax.rsqrt(s_hi * (1.0 / DIFF_QK) + RMS_EPS))
        return x * r * g2

    for h in range(DIFF_HEADS):
        sl = slice(h * LANES, (h + 1) * LANES)
        qn = half_rms(u[:, sl], qn_ref[...]) * scale
        q1_out[:, sl] = jnp.where(lo, qn, 0.0).astype(BF16)
        q2_out[:, sl] = jnp.where(lo, 0.0, qn).astype(BF16)
        k_out[:, sl] = half_rms(u[:, width + h * LANES:width + (h + 1) * LANES], kn_ref[...]).astype(BF16)
    vt_out[...] = u[:, 2 * width:].T.astype(BF16)


def _diff_prep(h, g, w, qn2, kn2, tm):
    m, d = h.shape
    width = DIFF_HEADS * LANES
    out = pl.BlockSpec((tm, width), lambda i: (i, 0))
    return pl.pallas_call(
        _diff_prep_kernel,
        grid=(m // tm,),
        in_specs=[pl.BlockSpec((tm, d), lambda i: (i, 0)), _full(g, 1), _full(w, 1), _full(qn2, 1), _full(kn2, 1)],
        out_specs=[out, out, out, pl.BlockSpec((None, width, tm), lambda i: (i, 0, 0))],
        out_shape=[jax.ShapeDtypeStruct((m, width), BF16)] * 3
        + [jax.ShapeDtypeStruct((m // tm, width, tm), BF16)],
        compiler_params=_params("parallel"),
        name="diff_prep",
    )(h, g, w, qn2, kn2)


def _nsa_prep_kernel(x_ref, g_ref, w_ref, qn_ref, ksn_ref, kwn_ref, q_out, k_out, raw_out, vst_out,
                     vwt_out, gt_out):
    xn = _rms_rows(x_ref[...], g_ref[...]).astype(BF16)
    u = _dot(xn, w_ref[...])
    scale = NSA_D ** -0.5 * LOG2E
    for h in range(NSA_HEADS):
        sl = slice(h * LANES, (h + 1) * LANES)
        q_out[:, sl] = (_rms_rows(u[:, sl], qn_ref[...]) * scale).astype(BF16)
    base = NSA_HEADS * LANES
    piece = lambda n: u[:, base + n * LANES:base + (n + 1) * LANES]
    raw_out[:, :LANES] = piece(0)
    raw_out[:, LANES:] = piece(1)
    k_out[:, :LANES] = _rms_rows(piece(2), ksn_ref[...]).astype(BF16)
    k_out[:, LANES:] = _rms_rows(piece(4), kwn_ref[...]).astype(BF16)
    vst_out[...] = piece(3).T.astype(BF16)
    vw_t = piece(5).T.astype(BF16)
    gate_t = jax.nn.sigmoid(piece(6)).T
    for c in range(vwt_out.shape[0]):
        tq = vwt_out.shape[2]
        vwt_out[c] = vw_t[:, c * tq:(c + 1) * tq]
        gt_out[c] = gate_t[:, c * tq:(c + 1) * tq]


def _nsa_prep(h, g, w, qn, ksn, kwn, tm, tq):
    m, d = h.shape
    nblk = tm // tq
    out = lambda c: pl.BlockSpec((tm, c), lambda i: (i, 0))
    blocks = pl.BlockSpec((nblk, LANES, tq), lambda i: (i, 0, 0))
    return pl.pallas_call(
        _nsa_prep_kernel,
        grid=(m // tm,),
        in_specs=[pl.BlockSpec((tm, d), lambda i: (i, 0))] + [_full(a, 1) for a in (g, w, qn, ksn, kwn)],
        out_specs=[out(NSA_HEADS * LANES), out(2 * LANES), out(2 * LANES),
                   pl.BlockSpec((None, NSA_D, tm), lambda i: (i, 0, 0)), blocks, blocks],
        out_shape=[jax.ShapeDtypeStruct((m, NSA_HEADS * LANES), BF16),
                   jax.ShapeDtypeStruct((m, 2 * LANES), BF16),
                   jax.ShapeDtypeStruct((m, 2 * LANES), F32),
                   jax.ShapeDtypeStruct((m // tm, NSA_D, tm), BF16),
                   jax.ShapeDtypeStruct((m // tq, LANES, tq), BF16),
                   jax.ShapeDtypeStruct((m // tq, LANES, tq), F32)],
        compiler_params=_params("parallel"),
        name="nsa_prep",
    )(h, g, w, qn, ksn, kwn)


def _nsa_compress_kernel(zk_ref, zv_ref, pek_ref, pev_ref, wk_ref, wv_ref, kcn_ref, kc_out, vct_out):
    cs = NSA_CMP_STRIDE
    nchunk = zk_ref.shape[0] // cs

    def compress(z_ref, pe_ref, w_ref):
        first = jnp.zeros((nchunk, NSA_D), F32)
        second = jnp.zeros((nchunk, NSA_D), F32)
        for l in range(cs):
            z = z_ref[pl.ds(l, nchunk, stride=cs), :]
            first += _dot((z + pe_ref[l:l + 1, :]).astype(BF16), w_ref[l * NSA_D:(l + 1) * NSA_D, :])
            second += _dot((z + pe_ref[cs + l:cs + l + 1, :]).astype(BF16),
                           w_ref[(cs + l) * NSA_D:(cs + l + 1) * NSA_D, :])
        return first + pltpu.roll(second, shift=nchunk - 1, axis=0)

    kc_out[...] = _rms_rows(compress(zk_ref, pek_ref, wk_ref), kcn_ref[...]).astype(BF16)
    vct_out[...] = compress(zv_ref, pev_ref, wv_ref).T.astype(BF16)


def _nsa_compress(raw, pek, pev, wk, wv, kcn):
    b, s, _ = raw.shape
    nchunk = s // NSA_CMP_STRIDE
    return pl.pallas_call(
        _nsa_compress_kernel,
        grid=(b,),
        in_specs=[pl.BlockSpec((None, s, NSA_D), lambda bi: (bi, 0, 0)),
                  pl.BlockSpec((None, s, NSA_D), lambda bi: (bi, 0, 1))]
        + [_full(a, 1) for a in (pek, pev, wk, wv, kcn)],
        out_specs=[pl.BlockSpec((None, nchunk, NSA_D), lambda bi: (bi, 0, 0)),
                   pl.BlockSpec((None, NSA_D, nchunk), lambda bi: (bi, 0, 0))],
        out_shape=[jax.ShapeDtypeStruct((b, nchunk, NSA_D), BF16),
                   jax.ShapeDtypeStruct((b, NSA_D, nchunk), BF16)],
        compiler_params=_params("parallel"),
        name="nsa_compress",
    )(raw, raw, pek, pev, wk, wv, kcn)


def _softmax_step(s, vt, m, l, acc):
    m_new = jnp.maximum(m, jnp.max(s, axis=0, keepdims=True))
    alpha = jnp.exp2(m - m_new)
    p = jnp.exp2(s - m_new)
    l = alpha * l + jnp.sum(p, axis=0, keepdims=True)
    acc = alpha * acc + _dot(vt, p.astype(BF16))
    return m_new, l, acc


def _softmax_init(nq, dv):
    return (jnp.full((1, nq), NEG_INF, F32), jnp.zeros((1, nq), F32), jnp.zeros((dv, nq), F32))


def _softmax_out(l, acc):
    return acc * (1.0 / jnp.maximum(l, 1e-30))


def _softmax_exact(s, mask):
    m = jnp.max(s, axis=0, keepdims=True)
    e = jnp.where(mask, jnp.exp2(s - m), 0.0)
    return e * (1.0 / jnp.maximum(jnp.sum(e, axis=0, keepdims=True), 1e-30))


def _causal_mask(tk, tq, d, strict=False, reps=1):
    key = d * tk + lax.broadcasted_iota(jnp.int32, (tk, reps * tq), 0)
    qry = lax.broadcasted_iota(jnp.int32, (tk, reps * tq), 1)
    for _ in range(1, reps):
        qry = jnp.where(qry >= tq, qry - tq, qry)
    return key < qry if strict else key <= qry


def _lanes(parts):
    return jnp.concatenate(parts, axis=1)


def _mla_attn_kernel(q_ref, k_ref, vt_ref, on_ref, o_ref, *, tq, tk):
    i = pl.program_id(1)
    r = tq // tk
    dq = 2 * LANES
    qs = [q_ref[:, h * dq:(h + 1) * dq] for h in range(MLA_HEADS)]

    def step(j, carry, mask):
        st = pl.multiple_of(j * tk, tk)
        scores = lambda h: _dot_nt(k_ref[pl.ds(st, tk), h * dq:(h + 1) * dq], qs[h])
        out = ()
        s_next = scores(0)
        for h in range(MLA_HEADS):
            s, s_next = s_next, (scores(h + 1) if h + 1 < MLA_HEADS else None)
            if mask is not None:
                s = jnp.where(mask, s, NEG_INF)
            out += _softmax_step(s, vt_ref[j, h * MLA_V:(h + 1) * MLA_V, :], *carry[3 * h:3 * h + 3])
        return out

    c = lax.fori_loop(0, i * r, lambda j, c: step(j, c, None), _softmax_init(tq, MLA_V) * MLA_HEADS)
    for d in range(r):
        c = step(i * r + d, c, _causal_mask(tk, tq, d))
    for h in range(MLA_HEADS):
        o = _softmax_out(c[3 * h + 1], c[3 * h + 2])
        o_ref[:, h * MLA_V:(h + 1) * MLA_V] = _rms_cols(o, on_ref[...]).T.astype(o_ref.dtype)


def _mla_attn(q, k, vt, on, tq, tk):
    b, s, _ = q.shape
    return pl.pallas_call(
        functools.partial(_mla_attn_kernel, tq=tq, tk=tk),
        grid=(b, s // tq),
        in_specs=[pl.BlockSpec((None, tq, q.shape[2]), lambda bi, i: (bi, i, 0)),
                  pl.BlockSpec((None, s, k.shape[2]), lambda bi, i: (bi, 0, 0)),
                  pl.BlockSpec((None,) + vt.shape[1:], lambda bi, i: (bi, 0, 0, 0)),
                  _full(on, 2)],
        out_specs=pl.BlockSpec((None, tq, MLA_HEADS * MLA_V), lambda bi, i: (bi, i, 0)),
        out_shape=jax.ShapeDtypeStruct((b, s, MLA_HEADS * MLA_V), BF16),
        compiler_params=_params("parallel", "arbitrary"),
        name="mla_attn",
    )(q, k, vt, on)


def _diff_attn_kernel(q1_ref, q2_ref, k_ref, vt_ref, lq1_ref, lk1_ref, lq2_ref, lk2_ref,
                      sub_ref, o_ref, *, tq, tk, lambda_init):
    i = pl.program_id(1)
    r = tq // tk
    t0 = i * tq
    keyrel = lax.broadcasted_iota(jnp.int32, (tk, 1), 0)
    qs = [jnp.concatenate([q1_ref[:, h * LANES:(h + 1) * LANES], q2_ref[:, h * LANES:(h + 1) * LANES]], axis=0)
          for h in range(DIFF_HEADS)]

    def step(j, carry, mask):
        st = pl.multiple_of(j * tk, tk)
        rel = (keyrel + (j * tk - t0)).astype(F32)
        scores = lambda h: _dot_nt(k_ref[pl.ds(st, tk), h * LANES:(h + 1) * LANES], qs[h])
        out = ()
        s_next = scores(0)
        for h in range(DIFF_HEADS):
            sl = slice(h * LANES, (h + 1) * LANES)
            s, s_next = s_next, (scores(h + 1) if h + 1 < DIFF_HEADS else None)
            s = s + DIFF_SLOPES[h] * rel
            if mask is not None:
                s = jnp.where(mask, s, NEG_INF)
            out += _softmax_step(s, vt_ref[j, sl, :], *carry[3 * h:3 * h + 3])
        return out

    c = lax.fori_loop(0, i * r, lambda j, c: step(j, c, None), _softmax_init(2 * tq, DIFF_V) * DIFF_HEADS)
    for d in range(r):
        c = step(i * r + d, c, _causal_mask(tk, tq, d, reps=2))
    lam = (jnp.exp(jnp.sum(lq1_ref[...] * lk1_ref[...], axis=-1, keepdims=True))
           - jnp.exp(jnp.sum(lq2_ref[...] * lk2_ref[...], axis=-1, keepdims=True)) + lambda_init)
    for h in range(DIFF_HEADS):
        o12 = _softmax_out(c[3 * h + 1], c[3 * h + 2])
        o = o12[:, :tq] - lam * o12[:, tq:]
        o_ref[:, h * DIFF_V:(h + 1) * DIFF_V] = (
            _rms_cols(o, sub_ref[...]) * (1.0 - lambda_init)).T.astype(o_ref.dtype)


def _diff_attn(q1, q2, k, vt, lq1, lk1, lq2, lk2, sub, lambda_init, tq, tk):
    b, s, w = q1.shape
    qs = pl.BlockSpec((None, tq, w), lambda bi, i: (bi, i, 0))
    smalls = [lq1, lk1, lq2, lk2, sub]
    return pl.pallas_call(
        functools.partial(_diff_attn_kernel, tq=tq, tk=tk, lambda_init=lambda_init),
        grid=(b, s // tq),
        in_specs=[qs, qs, pl.BlockSpec((None, s, w), lambda bi, i: (bi, 0, 0)),
                  pl.BlockSpec((None,) + vt.shape[1:], lambda bi, i: (bi, 0, 0, 0))]
        + [_full(a, 2) for a in smalls],
        out_specs=pl.BlockSpec((None, tq, DIFF_HEADS * DIFF_V), lambda bi, i: (bi, i, 0)),
        out_shape=jax.ShapeDtypeStruct((b, s, DIFF_HEADS * DIFF_V), BF16),
        compiler_params=_params("parallel", "arbitrary"),
        name="diff_attn",
    )(q1, q2, k, vt, *smalls)


def _sb_attn_kernel(q_ref, k_ref, vt_ref, tri_ref, on_ref, o_ref, *, tq, tk):
    i = pl.program_id(1)
    r = tq // tk
    tri = tri_ref[...]
    tb = tri.shape[0]
    qs = [q_ref[:, h * SB_D:(h + 1) * SB_D] for h in range(SB_HEADS)]

    def step(j, carry, mask):
        st = pl.multiple_of(j * tk, tk)
        logits = lambda h: _dot_nt(k_ref[pl.ds(st, tk), h * SB_D:(h + 1) * SB_D], qs[h])
        out = ()
        z_next = logits(0)
        for h in range(SB_HEADS):
            run, acc = carry[2 * h:2 * h + 2]
            sl = slice(h * SB_D, (h + 1) * SB_D)
            z, z_next = z_next, (logits(h + 1) if h + 1 < SB_HEADS else None)
            log_sig = jnp.minimum(z, 0.0) - jnp.log2(1.0 + jnp.exp2(-jnp.abs(z)))
            log_not = log_sig - z
            if mask is not None:
                log_not = jnp.where(mask, log_not, 0.0)
            hi = log_not.astype(BF16)
            lo = (log_not - hi.astype(F32)).astype(BF16)
            after = []
            for blk in reversed(range(tk // tb)):
                rows = slice(blk * tb, (blk + 1) * tb)
                after.insert(0, _dot(tri, hi[rows]) + _dot(tri, lo[rows]) + run)
                run = run + jnp.sum(log_not[rows], axis=0, keepdims=True)
            a = jnp.exp2(log_sig + jnp.concatenate(after, axis=0))
            if mask is not None:
                a = jnp.where(mask, a, 0.0)
            acc = acc + _dot(vt_ref[j, sl, :], a.astype(BF16))
            out += (run, acc)
        return out

    carry = (jnp.zeros((1, tq), F32), jnp.zeros((SB_D, tq), F32)) * SB_HEADS
    for d in reversed(range(r)):
        carry = step(i * r + d, carry, _causal_mask(tk, tq, d, strict=True))
    carry = lax.fori_loop(0, i * r, lambda n, c: step(i * r - 1 - n, c, None), carry)
    for h in range(SB_HEADS):
        o_ref[:, h * SB_D:(h + 1) * SB_D] = _rms_cols(carry[2 * h + 1], on_ref[...]).T.astype(o_ref.dtype)


def _sb_attn(qk, vt, tri, on, tq, tk):
    b, s, _ = qk.shape
    w = SB_HEADS * SB_D
    return pl.pallas_call(
        functools.partial(_sb_attn_kernel, tq=tq, tk=tk),
        grid=(b, s // tq),
        in_specs=[pl.BlockSpec((None, tq, w), lambda bi, i: (bi, i, 0)),
                  pl.BlockSpec((None, s, w), lambda bi, i: (bi, 0, 1)),
                  pl.BlockSpec((None,) + vt.shape[1:], lambda bi, i: (bi, 0, 0, 0)),
                  _full(tri, 2), _full(on, 2)],
        out_specs=pl.BlockSpec((None, tq, w), lambda bi, i: (bi, i, 0)),
        out_shape=jax.ShapeDtypeStruct((b, s, w), BF16),
        compiler_params=_params("parallel", "arbitrary"),
        name="sb_attn",
    )(qk, qk, vt, tri, on)


def _topk_mask(score, n):
    blk = lax.broadcasted_iota(jnp.int32, score.shape, 0).astype(F32)

    def body(_, c):
        work, sel = c
        mx = jnp.max(work, axis=0, keepdims=True)
        idx = jnp.min(jnp.where(work == mx, blk, 1e9), axis=0, keepdims=True)
        pick = blk == idx
        return jnp.where(pick, -jnp.inf, work), jnp.where(pick, 1.0, sel)

    _, sel = lax.fori_loop(0, n, body, (score, jnp.zeros_like(score)), unroll=True)
    return sel


def _nsa_attn_kernel(q_ref, gt_ref, kc_ref, vct_ref, ks_ref, kw_ref, vst_ref, vwt_ref, ovt_ref, ext_ref,
                     on_ref, o_ref, selb_ref, *, t, tk, n_sb, n_sel):
    i = pl.program_id(1)
    t0 = i * t
    ncp = kc_ref.shape[0]
    nh = NSA_HEADS
    q4 = jnp.concatenate([q_ref[:, h * LANES:(h + 1) * LANES] for h in range(nh)], axis=0)
    qpos1 = t0 + lax.broadcasted_iota(jnp.int32, (1, t), 1)
    qpos4 = _lanes([qpos1] * nh)
    slope4 = _lanes([jnp.full((1, t), NSA_SLOPES[h], F32) for h in range(nh)])

    c_end = NSA_CMP_STRIDE * lax.broadcasted_iota(jnp.int32, (ncp, 1), 0) + (NSA_CMP_LEN - 1)
    cmask = c_end <= qpos4
    s = jnp.where(cmask, _dot_nt(kc_ref[...], q4) + slope4 * (c_end - t0).astype(F32), NEG_INF)
    p = _softmax_exact(s, cmask)
    o_cmp = _dot(vct_ref[...], p.astype(BF16))
    pc_sum = p[:, 0:t]
    for h in range(1, nh):
        pc_sum = pc_sum + p[:, h * t:(h + 1) * t]
    ovt = ovt_ref[...]
    hi = pc_sum.astype(BF16)
    r1 = pc_sum - hi.astype(F32)
    mid = r1.astype(BF16)
    lo = (r1 - mid.astype(F32)).astype(BF16)
    imp = _dot(ovt, hi) + _dot(ovt, mid) + _dot(ovt, lo)

    jb = lax.broadcasted_iota(jnp.int32, (ovt.shape[0], 1), 0)
    cur = qpos1 >> int(math.log2(NSA_SEL_BLOCK))
    valid = (jb * NSA_SEL_BLOCK <= qpos1) & (jb < n_sb)
    forced = (jb == 0) | (jb == cur) | (jb == cur - 1)
    score = jnp.where(valid, imp + jnp.where(forced, NSA_FORCE_BONUS, 0.0), NEG_INF)
    unselected = (jnp.where(score > 0.5 * NEG_INF, _topk_mask(score, n_sel), 0.0) - 1.0).astype(BF16)

    nwin = NSA_WINDOW // t + 1
    wblk = jnp.maximum(i - NSA_WINDOW // t, 0)
    w0 = pl.multiple_of(wblk * t, t)
    wcol = w0 + lax.broadcasted_iota(jnp.int32, (nwin * t, 1), 0)
    dist = qpos4 - wcol
    wmask = (dist >= 0) & (dist < NSA_WINDOW)
    s = _dot_nt(kw_ref[pl.ds(w0, nwin * t), :], q4) + slope4 * (wcol - t0).astype(F32)
    s = jnp.where(wmask, s, NEG_INF)
    e = jnp.exp2(s - jnp.max(s, axis=0, keepdims=True))
    p = (e * (1.0 / jnp.maximum(jnp.sum(e, axis=0, keepdims=True), 1e-30))).astype(BF16)
    o_win = _dot(vwt_ref[wblk], p[0:t])
    for c in range(1, nwin):
        o_win = o_win + _dot(vwt_ref[wblk + c], p[c * t:(c + 1) * t])

    for jj in range(selb_ref.shape[0]):
        selb_ref[jj] = _dot(ext_ref[jj * tk:(jj + 1) * tk, :], unselected)

    def sel_step(j, c, causal):
        st = pl.multiple_of(j * tk, tk)
        col = st + lax.broadcasted_iota(jnp.int32, (tk, 1), 0)
        s = (_dot_nt(ks_ref[pl.ds(st, tk), :], q4) + slope4 * (col - t0).astype(F32)
             + _lanes([selb_ref[j]] * nh))
        if causal:
            s = jnp.where(col <= qpos4, s, NEG_INF)
        return _softmax_step(s, vst_ref[j], *c)

    n_full = t0 // tk
    c = lax.fori_loop(0, n_full, lambda j, c: sel_step(j, c, False), _softmax_init(nh * t, NSA_D))
    _, l, acc = sel_step(n_full, c, True)
    o_sel = _softmax_out(l, acc)

    gates = gt_ref[...]
    for h in range(nh):
        cols = slice(h * t, (h + 1) * t)
        o = (gates[3 * h:3 * h + 1] * o_cmp[:, cols] + gates[3 * h + 1:3 * h + 2] * o_sel[:, cols]
             + gates[3 * h + 2:3 * h + 3] * o_win[:, cols])
        o_ref[:, h * LANES:(h + 1) * LANES] = _rms_cols(o, on_ref[...]).T.astype(o_ref.dtype)


def _nsa_attn(q, gt, kc, vct, kk, vst, vwt, ovt, ext, on, t, tk):
    b, s, _ = q.shape
    ncp = kc.shape[1]
    n_sb = s // NSA_SEL_BLOCK
    n_sel = min(NSA_N_SELECT, n_sb)
    k_spec = lambda n: pl.BlockSpec((None, s, NSA_D), lambda bi, i: (bi, 0, n))
    whole = lambda a: pl.BlockSpec((None,) + a.shape[1:], lambda bi, i: (bi,) + (0,) * (a.ndim - 1))
    return pl.pallas_call(
        functools.partial(_nsa_attn_kernel, t=t, tk=tk, n_sb=n_sb, n_sel=n_sel),
        grid=(b, s // t),
        in_specs=[pl.BlockSpec((None, t, NSA_HEADS * NSA_D), lambda bi, i: (bi, i, 0)),
                  pl.BlockSpec((None, None, LANES, t), lambda bi, i: (bi, i, 0, 0)),
                  whole(kc), whole(vct), k_spec(0), k_spec(1), whole(vst), whole(vwt),
                  _full(ovt, 2), _full(ext, 2), _full(on, 2)],
        out_specs=pl.BlockSpec((None, t, NSA_HEADS * NSA_D), lambda bi, i: (bi, i, 0)),
        out_shape=jax.ShapeDtypeStruct((b, s, NSA_HEADS * NSA_D), BF16),
        scratch_shapes=[pltpu.VMEM((s // tk, tk, t), F32)],
        compiler_params=_params("parallel", "arbitrary"),
        name="nsa_attn",
    )(q, gt, kc, vct, kk, kk, vst, vwt, ovt, ext, on)


def _pad_cols(w, width):
    return jnp.pad(w, ((0, 0), (0, width - w.shape[1])))


def _swap_halves(w):
    half = w.shape[-1] // 2
    return jnp.concatenate([w[..., half:], w[..., :half]], axis=-1)


def _row(v):
    return v.reshape(1, -1).astype(F32)


def _col(v):
    return v.reshape(-1, 1).astype(F32)


def _nsa_constants(s):
    ncp = s // NSA_CMP_STRIDE
    n_sb = s // NSA_SEL_BLOCK
    nb = -(-n_sb // 16) * 16
    c_start = NSA_CMP_STRIDE * np.arange(ncp)[None, :]
    sel_start = NSA_SEL_BLOCK * np.arange(nb)[:, None]
    overlap_t = ((c_start < sel_start + NSA_SEL_BLOCK) & (c_start + NSA_CMP_LEN > sel_start)
                 & (np.arange(nb)[:, None] < n_sb))
    expand_t = (np.arange(s)[:, None] // NSA_SEL_BLOCK) == np.arange(nb)[None, :]
    return jnp.asarray(overlap_t, BF16), jnp.asarray(expand_t * MASK_BIG, BF16)


def kernel(x, ffn1_norm, ffn1_w_gate, ffn1_w_up, ffn1_w_down, mix_norm, w_in, mla_cq_norm, mla_ckv_norm, mla_w_uq, mla_w_ukv, mla_qn_norm, mla_qr_norm, mla_kn_norm, mla_kr_norm, mla_o_norm, diff_q_norm, diff_k_norm, diff_lq1, diff_lk1, diff_lq2, diff_lk2, diff_subln, nsa_q_norm, nsa_pe_k, nsa_w_ck, nsa_pe_v, nsa_w_cv, nsa_kc_norm, nsa_ks_norm, nsa_kw_norm, nsa_o_norm, sb_o_norm, w_out, ffn2_norm, ffn2_w_gate, ffn2_w_up, ffn2_w_down):
    b, s, d = x.shape
    m = b * s
    depth = w_in.shape[0]

    pos = np.arange(s, dtype=np.float64)
    inv_freq = ROPE_THETA ** (-np.arange(0, MLA_ROPE, 2, dtype=np.float64) / MLA_ROPE)
    ang = pos[:, None] * inv_freq[None, :]
    cos, sin = np.cos(ang), np.sin(ang)
    zeros = np.zeros((s, LANES - MLA_ROPE))
    cos2 = jnp.asarray(np.concatenate([cos, cos, zeros], axis=-1), F32)
    sin2 = jnp.asarray(np.concatenate([-sin, sin, zeros], axis=-1), F32)
    tri = jnp.asarray(np.arange(SB_TRI)[None, :] > np.arange(SB_TRI)[:, None], BF16)
    overlap_t, expand_t = _nsa_constants(s)

    h = x.reshape(m, d)
    for l in range(depth):
        lambda_init = 0.8 - 0.6 * math.exp(-0.3 * l)
        h = _ffn(h, ffn1_norm[l], ffn1_w_gate[l].astype(BF16), ffn1_w_up[l].astype(BF16),
                 ffn1_w_down[l].astype(BF16))

        g_mix = _row(mix_norm[l])
        wl = w_in[l].astype(BF16)
        o0 = 0
        w_mla, o0 = wl[:, o0:o0 + W_MLA], o0 + W_MLA
        w_diff, o0 = wl[:, o0:o0 + W_DIFF], o0 + W_DIFF
        w_nsa, o0 = wl[:, o0:o0 + W_NSA], o0 + W_NSA
        w_sb = wl[:, o0:o0 + W_SB]
        per_batch = lambda a: a.reshape((b, a.shape[0] // b) + a.shape[1:])

        w_kr = w_mla[:, MLA_Q_RANK + MLA_KV_RANK:]
        w_mla_p = jnp.concatenate([w_mla[:, :MLA_Q_RANK + MLA_KV_RANK], _pad_cols(w_kr, LANES),
                                   _pad_cols(_swap_halves(w_kr), LANES)], axis=1)
        wuq = mla_w_uq[l].reshape(MLA_Q_RANK, MLA_HEADS, MLA_NOPE + MLA_ROPE)
        wuq_r = wuq[..., MLA_NOPE:]
        pad3 = lambda a: jnp.pad(a, ((0, 0), (0, 0), (0, LANES - a.shape[-1])))
        wuq_p = jnp.concatenate([wuq[..., :MLA_NOPE], pad3(wuq_r), pad3(_swap_halves(wuq_r))],
                                axis=-1).reshape(MLA_Q_RANK, MLA_HEADS * 3 * LANES).astype(BF16)
        padr = lambda v: _pad_cols(_row(v), LANES)
        q_mla, k_mla, vt_mla = _mla_prep(
            h.reshape(b, s, d), g_mix, w_mla_p, _row(mla_cq_norm[l]), _row(mla_ckv_norm[l]), wuq_p,
            mla_w_ukv[l].astype(BF16), _row(mla_qn_norm[l]), padr(mla_qr_norm[l]),
            padr(_swap_halves(mla_qr_norm[l])), _row(mla_kn_norm[l]), padr(mla_kr_norm[l]),
            padr(_swap_halves(mla_kr_norm[l])), cos2, sin2, tm=MLA_TK)
        o_a = _mla_attn(q_mla, k_mla, vt_mla, _col(mla_o_norm[l]), MLA_TQ, MLA_TK)

        two = lambda v: jnp.concatenate([_row(v), _row(v)], axis=1)
        q1, q2, k_d, vt_d = _diff_prep(h, g_mix, w_diff, two(diff_q_norm[l]),
                                       two(diff_k_norm[l]), tm=DIFF_TK)
        o_b = _diff_attn(per_batch(q1), per_batch(q2), per_batch(k_d), per_batch(vt_d), _row(diff_lq1[l]),
                         _row(diff_lk1[l]), _row(diff_lq2[l]), _row(diff_lk2[l]), _col(diff_subln[l]),
                         lambda_init, DIFF_TQ, DIFF_TK)

        w_nsa_p = _pad_cols(w_nsa, NSA_HEADS * LANES + 7 * LANES)
        q_n, k_n, raw_n, vst_n, vwt_n, gt_n = _nsa_prep(
            h, g_mix, w_nsa_p, _row(nsa_q_norm[l]), _row(nsa_ks_norm[l]), _row(nsa_kw_norm[l]), tm=NSA_TK, tq=NSA_TQ)
        kc, vct = _nsa_compress(per_batch(raw_n), nsa_pe_k[l], nsa_pe_v[l], nsa_w_ck[l].astype(BF16),
                                nsa_w_cv[l].astype(BF16), _row(nsa_kc_norm[l]))
        o_c = _nsa_attn(per_batch(q_n), per_batch(gt_n), kc, vct, per_batch(k_n), per_batch(vst_n),
                        per_batch(vwt_n), overlap_t, expand_t, _col(nsa_o_norm[l]), NSA_TQ, NSA_TK)

        qk_sb, vt_sb = _sb_prep(h, g_mix, w_sb, tm=SB_TK)
        o_d = _sb_attn(per_batch(qk_sb), per_batch(vt_sb), tri, _col(sb_o_norm[l]), SB_TQ, SB_TK)

        flat = lambda a: a.reshape(m, -1)
        h = _out_proj([flat(o_a), flat(o_b), flat(o_c), flat(o_d)], w_out[l].astype(BF16), h)
        h = _ffn(h, ffn2_norm[l], ffn2_w_gate[l].astype(BF16), ffn2_w_up[l].astype(BF16),
                 ffn2_w_down[l].astype(BF16))
    return h.reshape(b, s, d)
```

```python
import functools
import math

import numpy as np
import jax
import jax.numpy as jnp
from jax import lax
from jax.experimental import pallas as pl
from jax.experimental.pallas import tpu as pltpu

F32 = jnp.float32
BF16 = jnp.bfloat16

RMS_EPS = 1e-6
NEG_INF = -1e30
MASK_BIG = 2.0 ** 100
LANES = 128
VMEM_LIMIT = 56 * 1024 * 1024

MLA_HEADS, MLA_Q_RANK, MLA_KV_RANK, MLA_NOPE, MLA_ROPE, MLA_V = 4, 512, 256, 128, 64, 128
ROPE_THETA = 10000.0
DIFF_HEADS, DIFF_QK, DIFF_V = 4, 64, 128
NSA_HEADS, NSA_D = 4, 128
NSA_CMP_LEN, NSA_CMP_STRIDE, NSA_SEL_BLOCK, NSA_N_SELECT, NSA_WINDOW = 32, 16, 64, 16, 512
NSA_FORCE_BONUS = 1e4
SB_HEADS, SB_D = 4, 128
N_ALIBI_HEADS = DIFF_HEADS + NSA_HEADS
ALIBI_SLOPES = [2.0 ** (-8.0 * k / N_ALIBI_HEADS) for k in range(1, N_ALIBI_HEADS + 1)]
LOG2E = math.log2(math.e)
DIFF_SLOPES = [sl * LOG2E for sl in ALIBI_SLOPES[0::2]]
NSA_SLOPES = [sl * LOG2E for sl in ALIBI_SLOPES[1::2]]

W_MLA = MLA_Q_RANK + MLA_KV_RANK + MLA_ROPE
W_DIFF = DIFF_HEADS * (2 * DIFF_QK + 2 * DIFF_QK + DIFF_V)
W_NSA = NSA_HEADS * NSA_D + 6 * NSA_D + NSA_HEADS * 3
W_SB = 3 * SB_HEADS * SB_D

MLA_TQ, MLA_TK = 1024, 512
DIFF_TQ, DIFF_TK = 512, 512
SB_TQ, SB_TK = 1024, 512
SB_TRI = 256
NSA_TQ, NSA_TK = 256, 512


def _dot(a, b):
    return jnp.dot(a, b, preferred_element_type=F32)


def _dot_nt(a, b):
    return lax.dot_general(a, b, (((1,), (1,)), ((), ())), preferred_element_type=F32)


def _rms_rows(x, g):
    ms = jnp.mean(x * x, axis=-1, keepdims=True)
    return x * lax.rsqrt(ms + RMS_EPS) * g


def _rms_cols(x, g):
    ms = jnp.mean(x * x, axis=0, keepdims=True)
    return x * lax.rsqrt(ms + RMS_EPS) * g


def _params(*sem):
    return pltpu.CompilerParams(dimension_semantics=sem, vmem_limit_bytes=VMEM_LIMIT)


def _full(a, grid_rank):
    return pl.BlockSpec(a.shape, lambda *_: (0,) * a.ndim)


def _ffn_kernel(x_ref, g_ref, wg_ref, wu_ref, wd_ref, o_ref, xn_ref, acc_ref, *, nf):
    f = pl.program_id(1)

    @pl.when(f == 0)
    def _():
        xn_ref[...] = _rms_rows(x_ref[...], g_ref[...]).astype(BF16)
        acc_ref[...] = jnp.zeros_like(acc_ref)

    xn = xn_ref[...]
    gate = _dot(xn, wg_ref[...])
    up = _dot(xn, wu_ref[...])
    mid = (gate * jax.nn.sigmoid(gate) * up).astype(BF16)
    acc_ref[...] += _dot(mid, wd_ref[...])

    @pl.when(f == nf - 1)
    def _():
        o_ref[...] = x_ref[...] + 0.5 * acc_ref[...]


def _ffn(h, g, wg, wu, wd, tm=512, tf=512):
    m, d = h.shape
    f = wg.shape[1]
    return pl.pallas_call(
        functools.partial(_ffn_kernel, nf=f // tf),
        grid=(m // tm, f // tf),
        in_specs=[
            pl.BlockSpec((tm, d), lambda i, j: (i, 0)),
            pl.BlockSpec((1, d), lambda i, j: (0, 0)),
            pl.BlockSpec((d, tf), lambda i, j: (0, j)),
            pl.BlockSpec((d, tf), lambda i, j: (0, j)),
            pl.BlockSpec((tf, d), lambda i, j: (j, 0)),
        ],
        out_specs=pl.BlockSpec((tm, d), lambda i, j: (i, 0)),
        out_shape=jax.ShapeDtypeStruct((m, d), F32),
        scratch_shapes=[pltpu.VMEM((tm, d), BF16), pltpu.VMEM((tm, d), F32)],
        compiler_params=_params("parallel", "arbitrary"),
        name="ffn",
    )(h, g.reshape(1, d), wg, wu, wd)


def _out_proj_kernel(a0_ref, a1_ref, a2_ref, a3_ref, w_ref, r_ref, o_ref):
    acc = r_ref[...]
    for n, a_ref in enumerate((a0_ref, a1_ref, a2_ref, a3_ref)):
        kw = a_ref.shape[1]
        acc = acc + _dot(a_ref[...], w_ref[n * kw:(n + 1) * kw, :])
    o_ref[...] = acc


def _out_proj(parts, w, res, tm=512, tn=2048):
    m, d = res.shape
    kw = parts[0].shape[1]
    a_spec = pl.BlockSpec((tm, kw), lambda i, j: (i, 0))
    return pl.pallas_call(
        _out_proj_kernel,
        grid=(m // tm, d // tn),
        in_specs=[a_spec, a_spec, a_spec, a_spec,
                  pl.BlockSpec((w.shape[0], tn), lambda i, j: (0, j)),
                  pl.BlockSpec((tm, tn), lambda i, j: (i, j))],
        out_specs=pl.BlockSpec((tm, tn), lambda i, j: (i, j)),
        out_shape=jax.ShapeDtypeStruct((m, d), F32),
        compiler_params=_params("parallel", "arbitrary"),
        name="out_proj",
    )(*parts, w, res)


def _sb_prep_kernel(x_ref, g_ref, w_ref, qk_out, vt_out):
    xn = _rms_rows(x_ref[...], g_ref[...]).astype(BF16)
    u = _dot(xn, w_ref[...])
    width = SB_HEADS * SB_D
    qk_out[:, :width] = (u[:, :width] * (SB_D ** -0.5 * LOG2E)).astype(BF16)
    qk_out[:, width:] = u[:, width:2 * width].astype(BF16)
    vt_out[...] = u[:, 2 * width:].T.astype(BF16)


def _sb_prep(h, g, w, tm):
    m, d = h.shape
    width = SB_HEADS * SB_D
    return pl.pallas_call(
        _sb_prep_kernel,
        grid=(m // tm,),
        in_specs=[pl.BlockSpec((tm, d), lambda i: (i, 0)), _full(g, 1), _full(w, 1)],
        out_specs=[pl.BlockSpec((tm, 2 * width), lambda i: (i, 0)),
                   pl.BlockSpec((None, width, tm), lambda i: (i, 0, 0))],
        out_shape=[jax.ShapeDtypeStruct((m, 2 * width), BF16),
                   jax.ShapeDtypeStruct((m // tm, width, tm), BF16)],
        compiler_params=_params("parallel"),
        name="sb_prep",
    )(h, g, w)


def _rope_rows(x, x_sw, g, g_sw, cos2, sin2):
    ms = jnp.sum(x * x, axis=-1, keepdims=True) * (1.0 / MLA_ROPE)
    r = lax.rsqrt(ms + RMS_EPS)
    return (x * r * g) * cos2 + (x_sw * r * g_sw) * sin2


def _mla_prep_kernel(x_ref, g_ref, w_ref, cqn_ref, ckvn_ref, wuq_ref, wukv_ref, qn_ref, qr_ref,
                     qrs_ref, kn_ref, kr_ref, krs_ref, cos_ref, sin_ref, q_out, k_out, vt_out):
    xn = _rms_rows(x_ref[...], g_ref[...]).astype(BF16)
    u = _dot(xn, w_ref[...])
    c_q = u[:, :MLA_Q_RANK]
    c_kv = u[:, MLA_Q_RANK:MLA_Q_RANK + MLA_KV_RANK]
    kr = u[:, MLA_Q_RANK + MLA_KV_RANK:MLA_Q_RANK + MLA_KV_RANK + LANES]
    kr_sw = u[:, MLA_Q_RANK + MLA_KV_RANK + LANES:]
    cos2, sin2 = cos_ref[...], sin_ref[...]
    qall = _dot(_rms_rows(c_q, cqn_ref[...]).astype(BF16), wuq_ref[...])
    kvall = _dot(_rms_rows(c_kv, ckvn_ref[...]).astype(BF16), wukv_ref[...])
    k_rot = _rope_rows(kr, kr_sw, kr_ref[...], krs_ref[...], cos2, sin2).astype(BF16)
    scale = (MLA_NOPE + MLA_ROPE) ** -0.5 * LOG2E
    for h in range(MLA_HEADS):
        qh = qall[:, h * 3 * LANES:(h + 1) * 3 * LANES]
        q_nope = _rms_rows(qh[:, :LANES], qn_ref[...]) * scale
        q_rot = _rope_rows(qh[:, LANES:2 * LANES], qh[:, 2 * LANES:], qr_ref[...], qrs_ref[...],
                           cos2, sin2) * scale
        q_out[:, h * 2 * LANES:h * 2 * LANES + LANES] = q_nope.astype(BF16)
        q_out[:, h * 2 * LANES + LANES:(h + 1) * 2 * LANES] = q_rot.astype(BF16)
        kvh = kvall[:, h * 2 * LANES:(h + 1) * 2 * LANES]
        k_out[:, h * 2 * LANES:h * 2 * LANES + LANES] = _rms_rows(kvh[:, :LANES], kn_ref[...]).astype(BF16)
        k_out[:, h * 2 * LANES + LANES:(h + 1) * 2 * LANES] = k_rot
        vt_out[h * LANES:(h + 1) * LANES, :] = kvh[:, LANES:].T.astype(BF16)


def _mla_prep(h3, g, w, cqn, ckvn, wuq, wukv, qn, qr, qrs, kn, kr, krs, cos2, sin2, tm):
    b, s, d = h3.shape
    smalls = [g, w, cqn, ckvn, wuq, wukv, qn, qr, qrs, kn, kr, krs]
    tab = pl.BlockSpec((tm, LANES), lambda bi, i: (i, 0))
    out = lambda c: pl.BlockSpec((None, tm, c), lambda bi, i: (bi, i, 0))
    return pl.pallas_call(
        _mla_prep_kernel,
        grid=(b, s // tm),
        in_specs=[pl.BlockSpec((None, tm, d), lambda bi, i: (bi, i, 0))] + [_full(a, 2) for a in smalls] + [tab, tab],
        out_specs=[out(MLA_HEADS * 2 * LANES), out(MLA_HEADS * 2 * LANES),
                   pl.BlockSpec((None, None, MLA_HEADS * MLA_V, tm), lambda bi, i: (bi, i, 0, 0))],
        out_shape=[jax.ShapeDtypeStruct((b, s, MLA_HEADS * 2 * LANES), BF16),
                   jax.ShapeDtypeStruct((b, s, MLA_HEADS * 2 * LANES), BF16),
                   jax.ShapeDtypeStruct((b, s // tm, MLA_HEADS * MLA_V, tm), BF16)],
        compiler_params=_params("parallel", "parallel"),
        name="mla_prep",
    )(h3, *smalls, cos2, sin2)


def _diff_prep_kernel(x_ref, g_ref, w_ref, qn_ref, kn_ref, q1_out, q2_out, k_out, vt_out):
    xn = _rms_rows(x_ref[...], g_ref[...]).astype(BF16)
    u = _dot(xn, w_ref[...])
    width = DIFF_HEADS * LANES
    lane = lax.broadcasted_iota(jnp.int32, (1, LANES), 1)
    lo = lane < DIFF_QK
    scale = DIFF_QK ** -0.5 * LOG2E

    def half_rms(x, g2):
        sq = x * x
        s_lo = jnp.sum(jnp.where(lo, sq, 0.0), axis=-1, keepdims=True)
        s_hi = jnp.sum(jnp.where(lo, 0.0, sq), axis=-1, keepdims=True)
        r = jnp.where(lo, lax.rsqrt(s_lo * (1.0 / DIFF_QK) + RMS_EPS),
                      lax.rsqrt(s_hi * (1.0 / DIFF_QK) + RMS_EPS))
        return x * r * g2

    for h in range(DIFF_HEADS):
        sl = slice(h * LANES, (h + 1) * LANES)
        qn = half_rms(u[:, sl], qn_ref[...]) * scale
        q1_out[:, sl] = jnp.where(lo, qn, 0.0).astype(BF16)
        q2_out[:, sl] = jnp.where(lo, 0.0, qn).astype(BF16)
        k_out[:, sl] = half_rms(u[:, width + h * LANES:width + (h + 1) * LANES], kn_ref[...]).astype(BF16)
    vt_out[...] = u[:, 2 * width:].T.astype(BF16)


def _diff_prep(h, g, w, qn2, kn2, tm):
    m, d = h.shape
    width = DIFF_HEADS * LANES
    out = pl.BlockSpec((tm, width), lambda i: (i, 0))
    return pl.pallas_call(
        _diff_prep_kernel,
        grid=(m // tm,),
        in_specs=[pl.BlockSpec((tm, d), lambda i: (i, 0)), _full(g, 1), _full(w, 1), _full(qn2, 1), _full(kn2, 1)],
        out_specs=[out, out, out, pl.BlockSpec((None, width, tm), lambda i: (i, 0, 0))],
        out_shape=[jax.ShapeDtypeStruct((m, width), BF16)] * 3
        + [jax.ShapeDtypeStruct((m // tm, width, tm), BF16)],
        compiler_params=_params("parallel"),
        name="diff_prep",
    )(h, g, w, qn2, kn2)


def _nsa_prep_kernel(x_ref, g_ref, w_ref, qn_ref, ksn_ref, kwn_ref, q_out, k_out, raw_out, vst_out,
                     vwt_out, gt_out):
    xn = _rms_rows(x_ref[...], g_ref[...]).astype(BF16)
    u = _dot(xn, w_ref[...])
    scale = NSA_D ** -0.5 * LOG2E
    for h in range(NSA_HEADS):
        sl = slice(h * LANES, (h + 1) * LANES)
        q_out[:, sl] = (_rms_rows(u[:, sl], qn_ref[...]) * scale).astype(BF16)
    base = NSA_HEADS * LANES
    piece = lambda n: u[:, base + n * LANES:base + (n + 1) * LANES]
    raw_out[:, :LANES] = piece(0)
    raw_out[:, LANES:] = piece(1)
    k_out[:, :LANES] = _rms_rows(piece(2), ksn_ref[...]).astype(BF16)
    k_out[:, LANES:] = _rms_rows(piece(4), kwn_ref[...]).astype(BF16)
    vst_out[...] = piece(3).T.astype(BF16)
    vw_t = piece(5).T.astype(BF16)
    gate_t = jax.nn.sigmoid(piece(6)).T
    for c in range(vwt_out.shape[0]):
        tq = vwt_out.shape[2]
        vwt_out[c] = vw_t[:, c * tq:(c + 1) * tq]
        gt_out[c] = gate_t[:, c * tq:(c + 1) * tq]


def _nsa_prep(h, g, w, qn, ksn, kwn, tm, tq):
    m, d = h.shape
    nblk = tm // tq
    out = lambda c: pl.BlockSpec((tm, c), lambda i: (i, 0))
    blocks = pl.BlockSpec((nblk, LANES, tq), lambda i: (i, 0, 0))
    return pl.pallas_call(
        _nsa_prep_kernel,
        grid=(m // tm,),
        in_specs=[pl.BlockSpec((tm, d), lambda i: (i, 0))] + [_full(a, 1) for a in (g, w, qn, ksn, kwn)],
        out_specs=[out(NSA_HEADS * LANES), out(2 * LANES), out(2 * LANES),
                   pl.BlockSpec((None, NSA_D, tm), lambda i: (i, 0, 0)), blocks, blocks],
        out_shape=[jax.ShapeDtypeStruct((m, NSA_HEADS * LANES), BF16),
                   jax.ShapeDtypeStruct((m, 2 * LANES), BF16),
                   jax.ShapeDtypeStruct((m, 2 * LANES), F32),
                   jax.ShapeDtypeStruct((m // tm, NSA_D, tm), BF16),
                   jax.ShapeDtypeStruct((m // tq, LANES, tq), BF16),
                   jax.ShapeDtypeStruct((m // tq, LANES, tq), F32)],
        compiler_params=_params("parallel"),
        name="nsa_prep",
    )(h, g, w, qn, ksn, kwn)


def _nsa_compress_kernel(zk_ref, zv_ref, pek_ref, pev_ref, wk_ref, wv_ref, kcn_ref, kc_out, vct_out):
    cs = NSA_CMP_STRIDE
    nchunk = zk_ref.shape[0] // cs

    def compress(z_ref, pe_ref, w_ref):
        first = jnp.zeros((nchunk, NSA_D), F32)
        second = jnp.zeros((nchunk, NSA_D), F32)
        for l in range(cs):
            z = z_ref[pl.ds(l, nchunk, stride=cs), :]
            first += _dot((z + pe_ref[l:l + 1, :]).astype(BF16), w_ref[l * NSA_D:(l + 1) * NSA_D, :])
            second += _dot((z + pe_ref[cs + l:cs + l + 1, :]).astype(BF16),
                           w_ref[(cs + l) * NSA_D:(cs + l + 1) * NSA_D, :])
        return first + pltpu.roll(second, shift=nchunk - 1, axis=0)

    kc_out[...] = _rms_rows(compress(zk_ref, pek_ref, wk_ref), kcn_ref[...]).astype(BF16)
    vct_out[...] = compress(zv_ref, pev_ref, wv_ref).T.astype(BF16)


def _nsa_compress(raw, pek, pev, wk, wv, kcn):
    b, s, _ = raw.shape
    nchunk = s // NSA_CMP_STRIDE
    return pl.pallas_call(
        _nsa_compress_kernel,
        grid=(b,),
        in_specs=[pl.BlockSpec((None, s, NSA_D), lambda bi: (bi, 0, 0)),
                  pl.BlockSpec((None, s, NSA_D), lambda bi: (bi, 0, 1))]
        + [_full(a, 1) for a in (pek, pev, wk, wv, kcn)],
        out_specs=[pl.BlockSpec((None, nchunk, NSA_D), lambda bi: (bi, 0, 0)),
                   pl.BlockSpec((None, NSA_D, nchunk), lambda bi: (bi, 0, 0))],
        out_shape=[jax.ShapeDtypeStruct((b, nchunk, NSA_D), BF16),
                   jax.ShapeDtypeStruct((b, NSA_D, nchunk), BF16)],
        compiler_params=_params("parallel"),
        name="nsa_compress",
    )(raw, raw, pek, pev, wk, wv, kcn)


def _softmax_step(s, vt, m, l, acc):
    m_new = jnp.maximum(m, jnp.max(s, axis=0, keepdims=True))
    alpha = jnp.exp2(m - m_new)
    p = jnp.exp2(s - m_new)
    l = alpha * l + jnp.sum(p, axis=0, keepdims=True)
    acc = alpha * acc + _dot(vt, p.astype(BF16))
    return m_new, l, acc


def _softmax_init(nq, dv):
    return (jnp.full((1, nq), NEG_INF, F32), jnp.zeros((1, nq), F32), jnp.zeros((dv, nq), F32))


def _softmax_out(l, acc):
    return acc * (1.0 / jnp.maximum(l, 1e-30))


def _softmax_exact(s, mask):
    m = jnp.max(s, axis=0, keepdims=True)
    e = jnp.where(mask, jnp.exp2(s - m), 0.0)
    return e * (1.0 / jnp.maximum(jnp.sum(e, axis=0, keepdims=True), 1e-30))


def _causal_mask(tk, tq, d, strict=False, reps=1):
    key = d * tk + lax.broadcasted_iota(jnp.int32, (tk, reps * tq), 0)
    qry = lax.broadcasted_iota(jnp.int32, (tk, reps * tq), 1)
    for _ in range(1, reps):
        qry = jnp.where(qry >= tq, qry - tq, qry)
    return key < qry if strict else key <= qry


def _lanes(parts):
    return jnp.concatenate(parts, axis=1)


def _mla_attn_kernel(q_ref, k_ref, vt_ref, on_ref, o_ref, *, tq, tk):
    i = pl.program_id(1)
    r = tq // tk
    dq = 2 * LANES
    qs = [q_ref[:, h * dq:(h + 1) * dq] for h in range(MLA_HEADS)]

    def step(j, carry, mask):
        st = pl.multiple_of(j * tk, tk)
        scores = lambda h: _dot_nt(k_ref[pl.ds(st, tk), h * dq:(h + 1) * dq], qs[h])
        out = ()
        s_next = scores(0)
        for h in range(MLA_HEADS):
            s, s_next = s_next, (scores(h + 1) if h + 1 < MLA_HEADS else None)
            if mask is not None:
                s = jnp.where(mask, s, NEG_INF)
            out += _softmax_step(s, vt_ref[j, h * MLA_V:(h + 1) * MLA_V, :], *carry[3 * h:3 * h + 3])
        return out

    c = lax.fori_loop(0, i * r, lambda j, c: step(j, c, None), _softmax_init(tq, MLA_V) * MLA_HEADS)
    for d in range(r):
        c = step(i * r + d, c, _causal_mask(tk, tq, d))
    for h in range(MLA_HEADS):
        o = _softmax_out(c[3 * h + 1], c[3 * h + 2])
        o_ref[:, h * MLA_V:(h + 1) * MLA_V] = _rms_cols(o, on_ref[...]).T.astype(o_ref.dtype)


def _mla_attn(q, k, vt, on, tq, tk):
    b, s, _ = q.shape
    return pl.pallas_call(
        functools.partial(_mla_attn_kernel, tq=tq, tk=tk),
        grid=(b, s // tq),
        in_specs=[pl.BlockSpec((None, tq, q.shape[2]), lambda bi, i: (bi, i, 0)),
                  pl.BlockSpec((None, s, k.shape[2]), lambda bi, i: (bi, 0, 0)),
                  pl.BlockSpec((None,) + vt.shape[1:], lambda bi, i: (bi, 0, 0, 0)),
                  _full(on, 2)],
        out_specs=pl.BlockSpec((None, tq, MLA_HEADS * MLA_V), lambda bi, i: (bi, i, 0)),
        out_shape=jax.ShapeDtypeStruct((b, s, MLA_HEADS * MLA_V), BF16),
        compiler_params=_params("parallel", "arbitrary"),
        name="mla_attn",
    )(q, k, vt, on)


def _diff_attn_kernel(q1_ref, q2_ref, k_ref, vt_ref, lq1_ref, lk1_ref, lq2_ref, lk2_ref,
                      sub_ref, o_ref, *, tq, tk, lambda_init):
    i = pl.program_id(1)
    r = tq // tk
    t0 = i * tq
    keyrel = lax.broadcasted_iota(jnp.int32, (tk, 1), 0)
    qs = [jnp.concatenate([q1_ref[:, h * LANES:(h + 1) * LANES], q2_ref[:, h * LANES:(h + 1) * LANES]], axis=0)
          for h in range(DIFF_HEADS)]

    def step(j, carry, mask):
        st = pl.multiple_of(j * tk, tk)
        rel = (keyrel + (j * tk - t0)).astype(F32)
        scores = lambda h: _dot_nt(k_ref[pl.ds(st, tk), h * LANES:(h + 1) * LANES], qs[h])
        out = ()
        s_next = scores(0)
        for h in range(DIFF_HEADS):
            sl = slice(h * LANES, (h + 1) * LANES)
            s, s_next = s_next, (scores(h + 1) if h + 1 < DIFF_HEADS else None)
            s = s + DIFF_SLOPES[h] * rel
            if mask is not None:
                s = jnp.where(mask, s, NEG_INF)
            out += _softmax_step(s, vt_ref[j, sl, :], *carry[3 * h:3 * h + 3])
        return out

    c = lax.fori_loop(0, i * r, lambda j, c: step(j, c, None), _softmax_init(2 * tq, DIFF_V) * DIFF_HEADS)
    for d in range(r):
        c = step(i * r + d, c, _causal_mask(tk, tq, d, reps=2))
    lam = (jnp.exp(jnp.sum(lq1_ref[...] * lk1_ref[...], axis=-1, keepdims=True))
           - jnp.exp(jnp.sum(lq2_ref[...] * lk2_ref[...], axis=-1, keepdims=True)) + lambda_init)
    for h in range(DIFF_HEADS):
        o12 = _softmax_out(c[3 * h + 1], c[3 * h + 2])
        o = o12[:, :tq] - lam * o12[:, tq:]
        o_ref[:, h * DIFF_V:(h + 1) * DIFF_V] = (
            _rms_cols(o, sub_ref[...]) * (1.0 - lambda_init)).T.astype(o_ref.dtype)


def _diff_attn(q1, q2, k, vt, lq1, lk1, lq2, lk2, sub, lambda_init, tq, tk):
    b, s, w = q1.shape
    qs = pl.BlockSpec((None, tq, w), lambda bi, i: (bi, i, 0))
    smalls = [lq1, lk1, lq2, lk2, sub]
    return pl.pallas_call(
        functools.partial(_diff_attn_kernel, tq=tq, tk=tk, lambda_init=lambda_init),
        grid=(b, s // tq),
        in_specs=[qs, qs, pl.BlockSpec((None, s, w), lambda bi, i: (bi, 0, 0)),
                  pl.BlockSpec((None,) + vt.shape[1:], lambda bi, i: (bi, 0, 0, 0))]
        + [_full(a, 2) for a in smalls],
        out_specs=pl.BlockSpec((None, tq, DIFF_HEADS * DIFF_V), lambda bi, i: (bi, i, 0)),
        out_shape=jax.ShapeDtypeStruct((b, s, DIFF_HEADS * DIFF_V), BF16),
        compiler_params=_params("parallel", "arbitrary"),
        name="diff_attn",
    )(q1, q2, k, vt, *smalls)


def _sb_attn_kernel(q_ref, k_ref, vt_ref, tri_ref, on_ref, o_ref, *, tq, tk):
    i = pl.program_id(1)
    r = tq // tk
    tri = tri_ref[...]
    tb = tri.shape[0]
    qs = [q_ref[:, h * SB_D:(h + 1) * SB_D] for h in range(SB_HEADS)]

    def step(j, carry, mask):
        st = pl.multiple_of(j * tk, tk)
        logits = lambda h: _dot_nt(k_ref[pl.ds(st, tk), h * SB_D:(h + 1) * SB_D], qs[h])
        out = ()
        z_next = logits(0)
        for h in range(SB_HEADS):
            run, acc = carry[2 * h:2 * h + 2]
            sl = slice(h * SB_D, (h + 1) * SB_D)
            z, z_next = z_next, (logits(h + 1) if h + 1 < SB_HEADS else None)
            log_sig = jnp.minimum(z, 0.0) - jnp.log2(1.0 + jnp.exp2(-jnp.abs(z)))
            log_not = log_sig - z
            if mask is not None:
                log_not = jnp.where(mask, log_not, 0.0)
            hi = log_not.astype(BF16)
            lo = (log_not - hi.astype(F32)).astype(BF16)
            after = []
            for blk in reversed(range(tk // tb)):
                rows = slice(blk * tb, (blk + 1) * tb)
                after.insert(0, _dot(tri, hi[rows]) + _dot(tri, lo[rows]) + run)
                run = run + jnp.sum(log_not[rows], axis=0, keepdims=True)
            a = jnp.exp2(log_sig + jnp.concatenate(after, axis=0))
            if mask is not None:
                a = jnp.where(mask, a, 0.0)
            acc = acc + _dot(vt_ref[j, sl, :], a.astype(BF16))
            out += (run, acc)
        return out

    carry = (jnp.zeros((1, tq), F32), jnp.zeros((SB_D, tq), F32)) * SB_HEADS
    for d in reversed(range(r)):
        carry = step(i * r + d, carry, _causal_mask(tk, tq, d, strict=True))
    carry = lax.fori_loop(0, i * r, lambda n, c: step(i * r - 1 - n, c, None), carry)
    for h in range(SB_HEADS):
        o_ref[:, h * SB_D:(h + 1) * SB_D] = _rms_cols(carry[2 * h + 1], on_ref[...]).T.astype(o_ref.dtype)


def _sb_attn(qk, vt, tri, on, tq, tk):
    b, s, _ = qk.shape
    w = SB_HEADS * SB_D
    return pl.pallas_call(
        functools.partial(_sb_attn_kernel, tq=tq, tk=tk),
        grid=(b, s // tq),
        in_specs=[pl.BlockSpec((None, tq, w), lambda bi, i: (bi, i, 0)),
                  pl.BlockSpec((None, s, w), lambda bi, i: (bi, 0, 1)),
                  pl.BlockSpec((None,) + vt.shape[1:], lambda bi, i: (bi, 0, 0, 0)),
                  _full(tri, 2), _full(on, 2)],
        out_specs=pl.BlockSpec((None, tq, w), lambda bi, i: (bi, i, 0)),
        out_shape=jax.ShapeDtypeStruct((b, s, w), BF16),
        compiler_params=_params("parallel", "arbitrary"),
        name="sb_attn",
    )(qk, qk, vt, tri, on)


def _topk_mask(score, n):
    blk = lax.broadcasted_iota(jnp.int32, score.shape, 0).astype(F32)

    def body(_, c):
        work, sel = c
        mx = jnp.max(work, axis=0, keepdims=True)
        idx = jnp.min(jnp.where(work == mx, blk, 1e9), axis=0, keepdims=True)
        pick = blk == idx
        return jnp.where(pick, -jnp.inf, work), jnp.where(pick, 1.0, sel)

    _, sel = lax.fori_loop(0, n, body, (score, jnp.zeros_like(score)), unroll=True)
    return sel


def _nsa_attn_kernel(q_ref, gt_ref, kc_ref, vct_ref, ks_ref, kw_ref, vst_ref, vwt_ref, ovt_ref, ext_ref,
                     on_ref, o_ref, selb_ref, *, t, tk, n_sb, n_sel):
    i = pl.program_id(1)
    t0 = i * t
    ncp = kc_ref.shape[0]
    nh = NSA_HEADS
    q4 = jnp.concatenate([q_ref[:, h * LANES:(h + 1) * LANES] for h in range(nh)], axis=0)
    qpos1 = t0 + lax.broadcasted_iota(jnp.int32, (1, t), 1)
    qpos4 = _lanes([qpos1] * nh)
    slope4 = _lanes([jnp.full((1, t), NSA_SLOPES[h], F32) for h in range(nh)])

    c_end = NSA_CMP_STRIDE * lax.broadcasted_iota(jnp.int32, (ncp, 1), 0) + (NSA_CMP_LEN - 1)
    cmask = c_end <= qpos4
    s = jnp.where(cmask, _dot_nt(kc_ref[...], q4) + slope4 * (c_end - t0).astype(F32), NEG_INF)
    p = _softmax_exact(s, cmask)
    o_cmp = _dot(vct_ref[...], p.astype(BF16))
    pc_sum = p[:, 0:t]
    for h in range(1, nh):
        pc_sum = pc_sum + p[:, h * t:(h + 1) * t]
    ovt = ovt_ref[...]
    hi = pc_sum.astype(BF16)
    r1 = pc_sum - hi.astype(F32)
    mid = r1.astype(BF16)
    lo = (r1 - mid.astype(F32)).astype(BF16)
    imp = _dot(ovt, hi) + _dot(ovt, mid) + _dot(ovt, lo)

    jb = lax.broadcasted_iota(jnp.int32, (ovt.shape[0], 1), 0)
    cur = qpos1 >> int(math.log2(NSA_SEL_BLOCK))
    valid = (jb * NSA_SEL_BLOCK <= qpos1) & (jb < n_sb)
    forced = (jb == 0) | (jb == cur) | (jb == cur - 1)
    score = jnp.where(valid, imp + jnp.where(forced, NSA_FORCE_BONUS, 0.0), NEG_INF)
    unselected = (jnp.where(score > 0.5 * NEG_INF, _topk_mask(score, n_sel), 0.0) - 1.0).astype(BF16)

    nwin = NSA_WINDOW // t + 1
    wblk = jnp.maximum(i - NSA_WINDOW // t, 0)
    w0 = pl.multiple_of(wblk * t, t)
    wcol = w0 + lax.broadcasted_iota(jnp.int32, (nwin * t, 1), 0)
    dist = qpos4 - wcol
    wmask = (dist >= 0) & (dist < NSA_WINDOW)
    s = _dot_nt(kw_ref[pl.ds(w0, nwin * t), :], q4) + slope4 * (wcol - t0).astype(F32)
    s = jnp.where(wmask, s, NEG_INF)
    e = jnp.exp2(s - jnp.max(s, axis=0, keepdims=True))
    p = (e * (1.0 / jnp.maximum(jnp.sum(e, axis=0, keepdims=True), 1e-30))).astype(BF16)
    o_win = _dot(vwt_ref[wblk], p[0:t])
    for c in range(1, nwin):
        o_win = o_win + _dot(vwt_ref[wblk + c], p[c * t:(c + 1) * t])

    for jj in range(selb_ref.shape[0]):
        selb_ref[jj] = _dot(ext_ref[jj * tk:(jj + 1) * tk, :], unselected)

    def sel_step(j, c, causal):
        st = pl.multiple_of(j * tk, tk)
        col = st + lax.broadcasted_iota(jnp.int32, (tk, 1), 0)
        s = (_dot_nt(ks_ref[pl.ds(st, tk), :], q4) + slope4 * (col - t0).astype(F32)
             + _lanes([selb_ref[j]] * nh))
        if causal:
            s = jnp.where(col <= qpos4, s, NEG_INF)
        return _softmax_step(s, vst_ref[j], *c)

    n_full = t0 // tk
    c = lax.fori_loop(0, n_full, lambda j, c: sel_step(j, c, False), _softmax_init(nh * t, NSA_D))
    _, l, acc = sel_step(n_full, c, True)
    o_sel = _softmax_out(l, acc)

    gates = gt_ref[...]
    for h in range(nh):
        cols = slice(h * t, (h + 1) * t)
        o = (gates[3 * h:3 * h + 1] * o_cmp[:, cols] + gates[3 * h + 1:3 * h + 2] * o_sel[:, cols]
             + gates[3 * h + 2:3 * h + 3] * o_win[:, cols])
        o_ref[:, h * LANES:(h + 1) * LANES] = _rms_cols(o, on_ref[...]).T.astype(o_ref.dtype)


def _nsa_attn(q, gt, kc, vct, kk, vst, vwt, ovt, ext, on, t, tk):
    b, s, _ = q.shape
    ncp = kc.shape[1]
    n_sb = s // NSA_SEL_BLOCK
    n_sel = min(NSA_N_SELECT, n_sb)
    k_spec = lambda n: pl.BlockSpec((None, s, NSA_D), lambda bi, i: (bi, 0, n))
    whole = lambda a: pl.BlockSpec((None,) + a.shape[1:], lambda bi, i: (bi,) + (0,) * (a.ndim - 1))
    return pl.pallas_call(
        functools.partial(_nsa_attn_kernel, t=t, tk=tk, n_sb=n_sb, n_sel=n_sel),
        grid=(b, s // t),
        in_specs=[pl.BlockSpec((None, t, NSA_HEADS * NSA_D), lambda bi, i: (bi, i, 0)),
                  pl.BlockSpec((None, None, LANES, t), lambda bi, i: (bi, i, 0, 0)),
                  whole(kc), whole(vct), k_spec(0), k_spec(1), whole(vst), whole(vwt),
                  _full(ovt, 2), _full(ext, 2), _full(on, 2)],
        out_specs=pl.BlockSpec((None, t, NSA_HEADS * NSA_D), lambda bi, i: (bi, i, 0)),
        out_shape=jax.ShapeDtypeStruct((b, s, NSA_HEADS * NSA_D), BF16),
        scratch_shapes=[pltpu.VMEM((s // tk, tk, t), F32)],
        compiler_params=_params("parallel", "arbitrary"),
        name="nsa_attn",
    )(q, gt, kc, vct, kk, kk, vst, vwt, ovt, ext, on)


def _pad_cols(w, width):
    return jnp.pad(w, ((0, 0), (0, width - w.shape[1])))


def _swap_halves(w):
    half = w.shape[-1] // 2
    return jnp.concatenate([w[..., half:], w[..., :half]], axis=-1)


def _row(v):
    return v.reshape(1, -1).astype(F32)


def _col(v):
    return v.reshape(-1, 1).astype(F32)


def _nsa_constants(s):
    ncp = s // NSA_CMP_STRIDE
    n_sb = s // NSA_SEL_BLOCK
    nb = -(-n_sb // 16) * 16
    c_start = NSA_CMP_STRIDE * np.arange(ncp)[None, :]
    sel_start = NSA_SEL_BLOCK * np.arange(nb)[:, None]
    overlap_t = ((c_start < sel_start + NSA_SEL_BLOCK) & (c_start + NSA_CMP_LEN > sel_start)
                 & (np.arange(nb)[:, None] < n_sb))
    expand_t = (np.arange(s)[:, None] // NSA_SEL_BLOCK) == np.arange(nb)[None, :]
    return jnp.asarray(overlap_t, BF16), jnp.asarray(expand_t * MASK_BIG, BF16)


def kernel(x, ffn1_norm, ffn1_w_gate, ffn1_w_up, ffn1_w_down, mix_norm, w_in, mla_cq_norm, mla_ckv_norm, mla_w_uq, mla_w_ukv, mla_qn_norm, mla_qr_norm, mla_kn_norm, mla_kr_norm, mla_o_norm, diff_q_norm, diff_k_norm, diff_lq1, diff_lk1, diff_lq2, diff_lk2, diff_subln, nsa_q_norm, nsa_pe_k, nsa_w_ck, nsa_pe_v, nsa_w_cv, nsa_kc_norm, nsa_ks_norm, nsa_kw_norm, nsa_o_norm, sb_o_norm, w_out, ffn2_norm, ffn2_w_gate, ffn2_w_up, ffn2_w_down):
    b, s, d = x.shape
    m = b * s
    depth = w_in.shape[0]

    pos = np.arange(s, dtype=np.float64)
    inv_freq = ROPE_THETA ** (-np.arange(0, MLA_ROPE, 2, dtype=np.float64) / MLA_ROPE)
    ang = pos[:, None] * inv_freq[None, :]
    cos, sin = np.cos(ang), np.sin(ang)
    zeros = np.zeros((s, LANES - MLA_ROPE))
    cos2 = jnp.asarray(np.concatenate([cos, cos, zeros], axis=-1), F32)
    sin2 = jnp.asarray(np.concatenate([-sin, sin, zeros], axis=-1), F32)
    tri = jnp.asarray(np.arange(SB_TRI)[None, :] > np.arange(SB_TRI)[:, None], BF16)
    overlap_t, expand_t = _nsa_constants(s)

    h = x.reshape(m, d)
    for l in range(depth):
        lambda_init = 0.8 - 0.6 * math.exp(-0.3 * l)
        h = _ffn(h, ffn1_norm[l], ffn1_w_gate[l].astype(BF16), ffn1_w_up[l].astype(BF16),
                 ffn1_w_down[l].astype(BF16))

        g_mix = _row(mix_norm[l])
        wl = w_in[l].astype(BF16)
        o0 = 0
        w_mla, o0 = wl[:, o0:o0 + W_MLA], o0 + W_MLA
        w_diff, o0 = wl[:, o0:o0 + W_DIFF], o0 + W_DIFF
        w_nsa, o0 = wl[:, o0:o0 + W_NSA], o0 + W_NSA
        w_sb = wl[:, o0:o0 + W_SB]
        per_batch = lambda a: a.reshape((b, a.shape[0] // b) + a.shape[1:])

        w_kr = w_mla[:, MLA_Q_RANK + MLA_KV_RANK:]
        w_mla_p = jnp.concatenate([w_mla[:, :MLA_Q_RANK + MLA_KV_RANK], _pad_cols(w_kr, LANES),
                                   _pad_cols(_swap_halves(w_kr), LANES)], axis=1)
        wuq = mla_w_uq[l].reshape(MLA_Q_RANK, MLA_HEADS, MLA_NOPE + MLA_ROPE)
        wuq_r = wuq[..., MLA_NOPE:]
        pad3 = lambda a: jnp.pad(a, ((0, 0), (0, 0), (0, LANES - a.shape[-1])))
        wuq_p = jnp.concatenate([wuq[..., :MLA_NOPE], pad3(wuq_r), pad3(_swap_halves(wuq_r))],
                                axis=-1).reshape(MLA_Q_RANK, MLA_HEADS * 3 * LANES).astype(BF16)
        padr = lambda v: _pad_cols(_row(v), LANES)
        q_mla, k_mla, vt_mla = _mla_prep(
            h.reshape(b, s, d), g_mix, w_mla_p, _row(mla_cq_norm[l]), _row(mla_ckv_norm[l]), wuq_p,
            mla_w_ukv[l].astype(BF16), _row(mla_qn_norm[l]), padr(mla_qr_norm[l]),
            padr(_swap_halves(mla_qr_norm[l])), _row(mla_kn_norm[l]), padr(mla_kr_norm[l]),
            padr(_swap_halves(mla_kr_norm[l])), cos2, sin2, tm=MLA_TK)
        o_a = _mla_attn(q_mla, k_mla, vt_mla, _col(mla_o_norm[l]), MLA_TQ, MLA_TK)

        two = lambda v: jnp.concatenate([_row(v), _row(v)], axis=1)
        q1, q2, k_d, vt_d = _diff_prep(h, g_mix, w_diff, two(diff_q_norm[l]),
                                       two(diff_k_norm[l]), tm=DIFF_TK)
        o_b = _diff_attn(per_batch(q1), per_batch(q2), per_batch(k_d), per_batch(vt_d), _row(diff_lq1[l]),
                         _row(diff_lk1[l]), _row(diff_lq2[l]), _row(diff_lk2[l]), _col(diff_subln[l]),
                         lambda_init, DIFF_TQ, DIFF_TK)

        w_nsa_p = _pad_cols(w_nsa, NSA_HEADS * LANES + 7 * LANES)
        q_n, k_n, raw_n, vst_n, vwt_n, gt_n = _nsa_prep(
            h, g_mix, w_nsa_p, _row(nsa_q_norm[l]), _row(nsa_ks_norm[l]), _row(nsa_kw_norm[l]), tm=NSA_TK, tq=NSA_TQ)
        kc, vct = _nsa_compress(per_batch(raw_n), nsa_pe_k[l], nsa_pe_v[l], nsa_w_ck[l].astype(BF16),
                                nsa_w_cv[l].astype(BF16), _row(nsa_kc_norm[l]))
        o_c = _nsa_attn(per_batch(q_n), per_batch(gt_n), kc, vct, per_batch(k_n), per_batch(vst_n),
                        per_batch(vwt_n), overlap_t, expand_t, _col(nsa_o_norm[l]), NSA_TQ, NSA_TK)

        qk_sb, vt_sb = _sb_prep(h, g_mix, w_sb, tm=SB_TK)
        o_d = _sb_attn(per_batch(qk_sb), per_batch(vt_sb), tri, _col(sb_o_norm[l]), SB_TQ, SB_TK)

        flat = lambda a: a.reshape(m, -1)
        h = _out_proj([flat(o_a), flat(o_b), flat(o_c), flat(o_d)], w_out[l].astype(BF16), h)
        h = _ffn(h, ffn2_norm[l], ffn2_w_gate[l].astype(BF16), ffn2_w_up[l].astype(BF16),
                 ffn2_w_down[l].astype(BF16))
    return h.reshape(b, s, d)
```

```python
import functools
import math

import numpy as np
import jax
import jax.numpy as jnp
from jax import lax
from jax.experimental import pallas as pl
from jax.experimental.pallas import tpu as pltpu

F32 = jnp.float32
BF16 = jnp.bfloat16

RMS_EPS = 1e-6
NEG_INF = -1e30
MASK_BIG = 2.0 ** 100
LANES = 128
VMEM_LIMIT = 56 * 1024 * 1024

MLA_HEADS, MLA_Q_RANK, MLA_KV_RANK, MLA_NOPE, MLA_ROPE, MLA_V = 4, 512, 256, 128, 64, 128
ROPE_THETA = 10000.0
DIFF_HEADS, DIFF_QK, DIFF_V = 4, 64, 128
NSA_HEADS, NSA_D = 4, 128
NSA_CMP_LEN, NSA_CMP_STRIDE, NSA_SEL_BLOCK, NSA_N_SELECT, NSA_WINDOW = 32, 16, 64, 16, 512
NSA_FORCE_BONUS = 1e4
SB_HEADS, SB_D = 4, 128
N_ALIBI_HEADS = DIFF_HEADS + NSA_HEADS
ALIBI_SLOPES = [2.0 ** (-8.0 * k / N_ALIBI_HEADS) for k in range(1, N_ALIBI_HEADS + 1)]
LOG2E = math.log2(math.e)
DIFF_SLOPES = [sl * LOG2E for sl in ALIBI_SLOPES[0::2]]
NSA_SLOPES = [sl * LOG2E for sl in ALIBI_SLOPES[1::2]]

W_MLA = MLA_Q_RANK + MLA_KV_RANK + MLA_ROPE
W_DIFF = DIFF_HEADS * (2 * DIFF_QK + 2 * DIFF_QK + DIFF_V)
W_NSA = NSA_HEADS * NSA_D + 6 * NSA_D + NSA_HEADS * 3
W_SB = 3 * SB_HEADS * SB_D

MLA_TQ, MLA_TK = 1024, 512
DIFF_TQ, DIFF_TK = 512, 512
SB_TQ, SB_TK = 1024, 512
SB_TRI = 256
NSA_TQ, NSA_TK = 256, 512


def _dot(a, b):
    return jnp.dot(a, b, preferred_element_type=F32)


def _dot_nt(a, b):
    return lax.dot_general(a, b, (((1,), (1,)), ((), ())), preferred_element_type=F32)


def _rms_rows(x, g):
    ms = jnp.mean(x * x, axis=-1, keepdims=True)
    return x * lax.rsqrt(ms + RMS_EPS) * g


def _rms_cols(x, g):
    ms = jnp.mean(x * x, axis=0, keepdims=True)
    return x * lax.rsqrt(ms + RMS_EPS) * g


def _params(*sem):
    return pltpu.CompilerParams(dimension_semantics=sem, vmem_limit_bytes=VMEM_LIMIT)


def _full(a, grid_rank):
    return pl.BlockSpec(a.shape, lambda *_: (0,) * a.ndim)


def _ffn_kernel(x_ref, g_ref, wg_ref, wu_ref, wd_ref, o_ref, xn_ref, acc_ref, *, nf):
    f = pl.program_id(1)

    @pl.when(f == 0)
    def _():
        xn_ref[...] = _rms_rows(x_ref[...], g_ref[...]).astype(BF16)
        acc_ref[...] = jnp.zeros_like(acc_ref)

    xn = xn_ref[...]
    gate = _dot(xn, wg_ref[...])
    up = _dot(xn, wu_ref[...])
    mid = (gate * jax.nn.sigmoid(gate) * up).astype(BF16)
    acc_ref[...] += _dot(mid, wd_ref[...])

    @pl.when(f == nf - 1)
    def _():
        o_ref[...] = x_ref[...] + 0.5 * acc_ref[...]


def _ffn(h, g, wg, wu, wd, tm=512, tf=512):
    m, d = h.shape
    f = wg.shape[1]
    return pl.pallas_call(
        functools.partial(_ffn_kernel, nf=f // tf),
        grid=(m // tm, f // tf),
        in_specs=[
            pl.BlockSpec((tm, d), lambda i, j: (i, 0)),
            pl.BlockSpec((1, d), lambda i, j: (0, 0)),
            pl.BlockSpec((d, tf), lambda i, j: (0, j)),
            pl.BlockSpec((d, tf), lambda i, j: (0, j)),
            pl.BlockSpec((tf, d), lambda i, j: (j, 0)),
        ],
        out_specs=pl.BlockSpec((tm, d), lambda i, j: (i, 0)),
        out_shape=jax.ShapeDtypeStruct((m, d), F32),
        scratch_shapes=[pltpu.VMEM((tm, d), BF16), pltpu.VMEM((tm, d), F32)],
        compiler_params=_params("parallel", "arbitrary"),
        name="ffn",
    )(h, g.reshape(1, d), wg, wu, wd)


def _out_proj_kernel(a0_ref, a1_ref, a2_ref, a3_ref, w_ref, r_ref, o_ref):
    acc = r_ref[...]
    for n, a_ref in enumerate((a0_ref, a1_ref, a2_ref, a3_ref)):
        kw = a_ref.shape[1]
        acc = acc + _dot(a_ref[...], w_ref[n * kw:(n + 1) * kw, :])
    o_ref[...] = acc


def _out_proj(parts, w, res, tm=512, tn=2048):
    m, d = res.shape
    kw = parts[0].shape[1]
    a_spec = pl.BlockSpec((tm, kw), lambda i, j: (i, 0))
    return pl.pallas_call(
        _out_proj_kernel,
        grid=(m // tm, d // tn),
        in_specs=[a_spec, a_spec, a_spec, a_spec,
                  pl.BlockSpec((w.shape[0], tn), lambda i, j: (0, j)),
                  pl.BlockSpec((tm, tn), lambda i, j: (i, j))],
        out_specs=pl.BlockSpec((tm, tn), lambda i, j: (i, j)),
        out_shape=jax.ShapeDtypeStruct((m, d), F32),
        compiler_params=_params("parallel", "arbitrary"),
        name="out_proj",
    )(*parts, w, res)


def _sb_prep_kernel(x_ref, g_ref, w_ref, qk_out, vt_out):
    xn = _rms_rows(x_ref[...], g_ref[...]).astype(BF16)
    u = _dot(xn, w_ref[...])
    width = SB_HEADS * SB_D
    qk_out[:, :width] = (u[:, :width] * (SB_D ** -0.5 * LOG2E)).astype(BF16)
    qk_out[:, width:] = u[:, width:2 * width].astype(BF16)
    vt_out[...] = u[:, 2 * width:].T.astype(BF16)


def _sb_prep(h, g, w, tm):
    m, d = h.shape
    width = SB_HEADS * SB_D
    return pl.pallas_call(
        _sb_prep_kernel,
        grid=(m // tm,),
        in_specs=[pl.BlockSpec((tm, d), lambda i: (i, 0)), _full(g, 1), _full(w, 1)],
        out_specs=[pl.BlockSpec((tm, 2 * width), lambda i: (i, 0)),
                   pl.BlockSpec((None, width, tm), lambda i: (i, 0, 0))],
        out_shape=[jax.ShapeDtypeStruct((m, 2 * width), BF16),
                   jax.ShapeDtypeStruct((m // tm, width, tm), BF16)],
        compiler_params=_params("parallel"),
        name="sb_prep",
    )(h, g, w)


def _rope_rows(x, x_sw, g, g_sw, cos2, sin2):
    ms = jnp.sum(x * x, axis=-1, keepdims=True) * (1.0 / MLA_ROPE)
    r = lax.rsqrt(ms + RMS_EPS)
    return (x * r * g) * cos2 + (x_sw * r * g_sw) * sin2


def _mla_prep_kernel(x_ref, g_ref, w_ref, cqn_ref, ckvn_ref, wuq_ref, wukv_ref, qn_ref, qr_ref,
                     qrs_ref, kn_ref, kr_ref, krs_ref, cos_ref, sin_ref, q_out, k_out, vt_out):
    xn = _rms_rows(x_ref[...], g_ref[...]).astype(BF16)
    u = _dot(xn, w_ref[...])
    c_q = u[:, :MLA_Q_RANK]
    c_kv = u[:, MLA_Q_RANK:MLA_Q_RANK + MLA_KV_RANK]
    kr = u[:, MLA_Q_RANK + MLA_KV_RANK:MLA_Q_RANK + MLA_KV_RANK + LANES]
    kr_sw = u[:, MLA_Q_RANK + MLA_KV_RANK + LANES:]
    cos2, sin2 = cos_ref[...], sin_ref[...]
    qall = _dot(_rms_rows(c_q, cqn_ref[...]).astype(BF16), wuq_ref[...])
    kvall = _dot(_rms_rows(c_kv, ckvn_ref[...]).astype(BF16), wukv_ref[...])
    k_rot = _rope_rows(kr, kr_sw, kr_ref[...], krs_ref[...], cos2, sin2).astype(BF16)
    scale = (MLA_NOPE + MLA_ROPE) ** -0.5 * LOG2E
    for h in range(MLA_HEADS):
        qh = qall[:, h * 3 * LANES:(h + 1) * 3 * LANES]
        q_nope = _rms_rows(qh[:, :LANES], qn_ref[...]) * scale
        q_rot = _rope_rows(qh[:, LANES:2 * LANES], qh[:, 2 * LANES:], qr_ref[...], qrs_ref[...],
                           cos2, sin2) * scale
        q_out[:, h * 2 * LANES:h * 2 * LANES + LANES] = q_nope.astype(BF16)
        q_out[:, h * 2 * LANES + LANES:(h + 1) * 2 * LANES] = q_rot.astype(BF16)
        kvh = kvall[:, h * 2 * LANES:(h + 1) * 2 * LANES]
        k_out[:, h * 2 * LANES:h * 2 * LANES + LANES] = _rms_rows(kvh[:, :LANES], kn_ref[...]).astype(BF16)
        k_out[:, h * 2 * LANES + LANES:(h + 1) * 2 * LANES] = k_rot
        vt_out[h * LANES:(h + 1) * LANES, :] = kvh[:, LANES:].T.astype(BF16)


def _mla_prep(h3, g, w, cqn, ckvn, wuq, wukv, qn, qr, qrs, kn, kr, krs, cos2, sin2, tm):
    b, s, d = h3.shape
    smalls = [g, w, cqn, ckvn, wuq, wukv, qn, qr, qrs, kn, kr, krs]
    tab = pl.BlockSpec((tm, LANES), lambda bi, i: (i, 0))
    out = lambda c: pl.BlockSpec((None, tm, c), lambda bi, i: (bi, i, 0))
    return pl.pallas_call(
        _mla_prep_kernel,
        grid=(b, s // tm),
        in_specs=[pl.BlockSpec((None, tm, d), lambda bi, i: (bi, i, 0))] + [_full(a, 2) for a in smalls] + [tab, tab],
        out_specs=[out(MLA_HEADS * 2 * LANES), out(MLA_HEADS * 2 * LANES),
                   pl.BlockSpec((None, None, MLA_HEADS * MLA_V, tm), lambda bi, i: (bi, i, 0, 0))],
        out_shape=[jax.ShapeDtypeStruct((b, s, MLA_HEADS * 2 * LANES), BF16),
                   jax.ShapeDtypeStruct((b, s, MLA_HEADS * 2 * LANES), BF16),
                   jax.ShapeDtypeStruct((b, s // tm, MLA_HEADS * MLA_V, tm), BF16)],
        compiler_params=_params("parallel", "parallel"),
        name="mla_prep",
    )(h3, *smalls, cos2, sin2)


def _diff_prep_kernel(x_ref, g_ref, w_ref, qn_ref, kn_ref, q1_out, q2_out, k_out, vt_out):
    xn = _rms_rows(x_ref[...], g_ref[...]).astype(BF16)
    u = _dot(xn, w_ref[...])
    width = DIFF_HEADS * LANES
    lane = lax.broadcasted_iota(jnp.int32, (1, LANES), 1)
    lo = lane < DIFF_QK
    scale = DIFF_QK ** -0.5 * LOG2E

    def half_rms(x, g2):
        sq = x * x
        s_lo = jnp.sum(jnp.where(lo, sq, 0.0), axis=-1, keepdims=True)
        s_hi = jnp.sum(jnp.where(lo, 0.0, sq), axis=-1, keepdims=True)
        r = jnp.where(lo, lax.rsqrt(s_lo * (1.0 / DIFF_QK) + RMS_EPS),
                      lax.rsqrt(s_hi * (1.0 / DIFF_QK) + RMS_EPS))
        return x * r * g2

    for h in range(DIFF_HEADS):
        sl = slice(h * LANES, (h + 1) * LANES)
        qn = half_rms(u[:, sl], qn_ref[...]) * scale
        q1_out[:, sl] = jnp.where(lo, qn, 0.0).astype(BF16)
        q2_out[:, sl] = jnp.where(lo, 0.0, qn).astype(BF16)
        k_out[:, sl] = half_rms(u[:, width + h * LANES:width + (h + 1) * LANES], kn_ref[...]).astype(BF16)
    vt_out[...] = u[:, 2 * width:].T.astype(BF16)


def _diff_prep(h, g, w, qn2, kn2, tm):
    m, d = h.shape
    width = DIFF_HEADS * LANES
    out = pl.BlockSpec((tm, width), lambda i: (i, 0))
    return pl.pallas_call(
        _diff_prep_kernel,
        grid=(m // tm,),
        in_specs=[pl.BlockSpec((tm, d), lambda i: (i, 0)), _full(g, 1), _full(w, 1), _full(qn2, 1), _full(kn2, 1)],
        out_specs=[out, out, out, pl.BlockSpec((None, width, tm), lambda i: (i, 0, 0))],
        out_shape=[jax.ShapeDtypeStruct((m, width), BF16)] * 3
        + [jax.ShapeDtypeStruct((m // tm, width, tm), BF16)],
        compiler_params=_params("parallel"),
        name="diff_prep",
    )(h, g, w, qn2, kn2)


def _nsa_prep_kernel(x_ref, g_ref, w_ref, qn_ref, ksn_ref, kwn_ref, q_out, k_out, raw_out, vst_out,
                     vwt_out, gt_out):
    xn = _rms_rows(x_ref[...], g_ref[...]).astype(BF16)
    u = _dot(xn, w_ref[...])
    scale = NSA_D ** -0.5 * LOG2E
    for h in range(NSA_HEADS):
        sl = slice(h * LANES, (h + 1) * LANES)
        q_out[:, sl] = (_rms_rows(u[:, sl], qn_ref[...]) * scale).astype(BF16)
    base = NSA_HEADS * LANES
    piece = lambda n: u[:, base + n * LANES:base + (n + 1) * LANES]
    raw_out[:, :LANES] = piece(0)
    raw_out[:, LANES:] = piece(1)
    k_out[:, :LANES] = _rms_rows(piece(2), ksn_ref[...]).astype(BF16)
    k_out[:, LANES:] = _rms_rows(piece(4), kwn_ref[...]).astype(BF16)
    vst_out[...] = piece(3).T.astype(BF16)
    vw_t = piece(5).T.astype(BF16)
    gate_t = jax.nn.sigmoid(piece(6)).T
    for c in range(vwt_out.shape[0]):
        tq = vwt_out.shape[2]
        vwt_out[c] = vw_t[:, c * tq:(c + 1) * tq]
        gt_out[c] = gate_t[:, c * tq:(c + 1) * tq]


def _nsa_prep(h, g, w, qn, ksn, kwn, tm, tq):
    m, d = h.shape
    nblk = tm // tq
    out = lambda c: pl.BlockSpec((tm, c), lambda i: (i, 0))
    blocks = pl.BlockSpec((nblk, LANES, tq), lambda i: (i, 0, 0))
    return pl.pallas_call(
        _nsa_prep_kernel,
        grid=(m // tm,),
        in_specs=[pl.BlockSpec((tm, d), lambda i: (i, 0))] + [_full(a, 1) for a in (g, w, qn, ksn, kwn)],
        out_specs=[out(NSA_HEADS * LANES), out(2 * LANES), out(2 * LANES),
                   pl.BlockSpec((None, NSA_D, tm), lambda i: (i, 0, 0)), blocks, blocks],
        out_shape=[jax.ShapeDtypeStruct((m, NSA_HEADS * LANES), BF16),
                   jax.ShapeDtypeStruct((m, 2 * LANES), BF16),
                   jax.ShapeDtypeStruct((m, 2 * LANES), F32),
                   jax.ShapeDtypeStruct((m // tm, NSA_D, tm), BF16),
                   jax.ShapeDtypeStruct((m // tq, LANES, tq), BF16),
                   jax.ShapeDtypeStruct((m // tq, LANES, tq), F32)],
        compiler_params=_params("parallel"),
        name="nsa_prep",
    )(h, g, w, qn, ksn, kwn)


def _nsa_compress_kernel(zk_ref, zv_ref, pek_ref, pev_ref, wk_ref, wv_ref, kcn_ref, kc_out, vct_out):
    cs = NSA_CMP_STRIDE
    nchunk = zk_ref.shape[0] // cs

    def compress(z_ref, pe_ref, w_ref):
        first = jnp.zeros((nchunk, NSA_D), F32)
        second = jnp.zeros((nchunk, NSA_D), F32)
        for l in range(cs):
            z = z_ref[pl.ds(l, nchunk, stride=cs), :]
            first += _dot((z + pe_ref[l:l + 1, :]).astype(BF16), w_ref[l * NSA_D:(l + 1) * NSA_D, :])
            second += _dot((z + pe_ref[cs + l:cs + l + 1, :]).astype(BF16),
                           w_ref[(cs + l) * NSA_D:(cs + l + 1) * NSA_D, :])
        return first + pltpu.roll(second, shift=nchunk - 1, axis=0)

    kc_out[...] = _rms_rows(compress(zk_ref, pek_ref, wk_ref), kcn_ref[...]).astype(BF16)
    vct_out[...] = compress(zv_ref, pev_ref, wv_ref).T.astype(BF16)


def _nsa_compress(raw, pek, pev, wk, wv, kcn):
    b, s, _ = raw.shape
    nchunk = s // NSA_CMP_STRIDE
    return pl.pallas_call(
        _nsa_compress_kernel,
        grid=(b,),
        in_specs=[pl.BlockSpec((None, s, NSA_D), lambda bi: (bi, 0, 0)),
                  pl.BlockSpec((None, s, NSA_D), lambda bi: (bi, 0, 1))]
        + [_full(a, 1) for a in (pek, pev, wk, wv, kcn)],
        out_specs=[pl.BlockSpec((None, nchunk, NSA_D), lambda bi: (bi, 0, 0)),
                   pl.BlockSpec((None, NSA_D, nchunk), lambda bi: (bi, 0, 0))],
        out_shape=[jax.ShapeDtypeStruct((b, nchunk, NSA_D), BF16),
                   jax.ShapeDtypeStruct((b, NSA_D, nchunk), BF16)],
        compiler_params=_params("parallel"),
        name="nsa_compress",
    )(raw, raw, pek, pev, wk, wv, kcn)


def _softmax_step(s, vt, m, l, acc):
    m_new = jnp.maximum(m, jnp.max(s, axis=0, keepdims=True))
    alpha = jnp.exp2(m - m_new)
    p = jnp.exp2(s - m_new)
    l = alpha * l + jnp.sum(p, axis=0, keepdims=True)
    acc = alpha * acc + _dot(vt, p.astype(BF16))
    return m_new, l, acc


def _softmax_init(nq, dv):
    return (jnp.full((1, nq), NEG_INF, F32), jnp.zeros((1, nq), F32), jnp.zeros((dv, nq), F32))


def _softmax_out(l, acc):
    return acc * (1.0 / jnp.maximum(l, 1e-30))


def _softmax_exact(s, mask):
    m = jnp.max(s, axis=0, keepdims=True)
    e = jnp.where(mask, jnp.exp2(s - m), 0.0)
    return e * (1.0 / jnp.maximum(jnp.sum(e, axis=0, keepdims=True), 1e-30))


def _causal_mask(tk, tq, d, strict=False, reps=1):
    key = d * tk + lax.broadcasted_iota(jnp.int32, (tk, reps * tq), 0)
    qry = lax.broadcasted_iota(jnp.int32, (tk, reps * tq), 1)
    for _ in range(1, reps):
        qry = jnp.where(qry >= tq, qry - tq, qry)
    return key < qry if strict else key <= qry


def _lanes(parts):
    return jnp.concatenate(parts, axis=1)


def _mla_attn_kernel(q_ref, k_ref, vt_ref, on_ref, o_ref, *, tq, tk):
    i = pl.program_id(1)
    r = tq // tk
    dq = 2 * LANES
    qs = [q_ref[:, h * dq:(h + 1) * dq] for h in range(MLA_HEADS)]

    def step(j, carry, mask, lo=0):
        st = pl.multiple_of(j * tk, tk)
        scores = lambda h: _dot_nt(k_ref[pl.ds(st, tk), h * dq:(h + 1) * dq], qs[h][lo:])
        out = ()
        s_next = scores(0)
        for h in range(MLA_HEADS):
            s, s_next = s_next, (scores(h + 1) if h + 1 < MLA_HEADS else None)
            if mask is not None:
                s = jnp.where(mask[:, lo:], s, NEG_INF)
            old = carry[3 * h:3 * h + 3]
            new = _softmax_step(s, vt_ref[j, h * MLA_V:(h + 1) * MLA_V, :], *[a[:, lo:] for a in old])
            out += tuple(_lanes([a[:, :lo], b]) for a, b in zip(old, new)) if lo else new
        return out

    c = lax.fori_loop(0, i * r, lambda j, c: step(j, c, None), _softmax_init(tq, MLA_V) * MLA_HEADS)
    for d in range(r):
        c = step(i * r + d, c, _causal_mask(tk, tq, d), lo=d * tk)
    for h in range(MLA_HEADS):
        o = _softmax_out(c[3 * h + 1], c[3 * h + 2])
        o_ref[:, h * MLA_V:(h + 1) * MLA_V] = _rms_cols(o, on_ref[...]).T.astype(o_ref.dtype)


def _mla_attn(q, k, vt, on, tq, tk):
    b, s, _ = q.shape
    return pl.pallas_call(
        functools.partial(_mla_attn_kernel, tq=tq, tk=tk),
        grid=(b, s // tq),
        in_specs=[pl.BlockSpec((None, tq, q.shape[2]), lambda bi, i: (bi, i, 0)),
                  pl.BlockSpec((None, s, k.shape[2]), lambda bi, i: (bi, 0, 0)),
                  pl.BlockSpec((None,) + vt.shape[1:], lambda bi, i: (bi, 0, 0, 0)),
                  _full(on, 2)],
        out_specs=pl.BlockSpec((None, tq, MLA_HEADS * MLA_V), lambda bi, i: (bi, i, 0)),
        out_shape=jax.ShapeDtypeStruct((b, s, MLA_HEADS * MLA_V), BF16),
        compiler_params=_params("parallel", "arbitrary"),
        name="mla_attn",
    )(q, k, vt, on)


def _diff_attn_kernel(q1_ref, q2_ref, k_ref, vt_ref, lq1_ref, lk1_ref, lq2_ref, lk2_ref,
                      sub_ref, o_ref, *, tq, tk, lambda_init):
    i = pl.program_id(1)
    r = tq // tk
    t0 = i * tq
    keyrel = lax.broadcasted_iota(jnp.int32, (tk, 1), 0)
    qs = [jnp.concatenate([q1_ref[:, h * LANES:(h + 1) * LANES], q2_ref[:, h * LANES:(h + 1) * LANES]], axis=0)
          for h in range(DIFF_HEADS)]

    def step(j, carry, mask):
        st = pl.multiple_of(j * tk, tk)
        rel = (keyrel + (j * tk - t0)).astype(F32)
        scores = lambda h: _dot_nt(k_ref[pl.ds(st, tk), h * LANES:(h + 1) * LANES], qs[h])
        out = ()
        s_next = scores(0)
        for h in range(DIFF_HEADS):
            sl = slice(h * LANES, (h + 1) * LANES)
            s, s_next = s_next, (scores(h + 1) if h + 1 < DIFF_HEADS else None)
            s = s + DIFF_SLOPES[h] * rel
            if mask is not None:
                s = jnp.where(mask, s, NEG_INF)
            out += _softmax_step(s, vt_ref[j, sl, :], *carry[3 * h:3 * h + 3])
        return out

    c = lax.fori_loop(0, i * r, lambda j, c: step(j, c, None), _softmax_init(2 * tq, DIFF_V) * DIFF_HEADS)
    for d in range(r):
        c = step(i * r + d, c, _causal_mask(tk, tq, d, reps=2))
    lam = (jnp.exp(jnp.sum(lq1_ref[...] * lk1_ref[...], axis=-1, keepdims=True))
           - jnp.exp(jnp.sum(lq2_ref[...] * lk2_ref[...], axis=-1, keepdims=True)) + lambda_init)
    for h in range(DIFF_HEADS):
        o12 = _softmax_out(c[3 * h + 1], c[3 * h + 2])
        o = o12[:, :tq] - lam * o12[:, tq:]
        o_ref[:, h * DIFF_V:(h + 1) * DIFF_V] = (
            _rms_cols(o, sub_ref[...]) * (1.0 - lambda_init)).T.astype(o_ref.dtype)


def _diff_attn(q1, q2, k, vt, lq1, lk1, lq2, lk2, sub, lambda_init, tq, tk):
    b, s, w = q1.shape
    qs = pl.BlockSpec((None, tq, w), lambda bi, i: (bi, i, 0))
    smalls = [lq1, lk1, lq2, lk2, sub]
    return pl.pallas_call(
        functools.partial(_diff_attn_kernel, tq=tq, tk=tk, lambda_init=lambda_init),
        grid=(b, s // tq),
        in_specs=[qs, qs, pl.BlockSpec((None, s, w), lambda bi, i: (bi, 0, 0)),
                  pl.BlockSpec((None,) + vt.shape[1:], lambda bi, i: (bi, 0, 0, 0))]
        + [_full(a, 2) for a in smalls],
        out_specs=pl.BlockSpec((None, tq, DIFF_HEADS * DIFF_V), lambda bi, i: (bi, i, 0)),
        out_shape=jax.ShapeDtypeStruct((b, s, DIFF_HEADS * DIFF_V), BF16),
        compiler_params=_params("parallel", "arbitrary"),
        name="diff_attn",
    )(q1, q2, k, vt, *smalls)


def _sb_attn_kernel(q_ref, k_ref, vt_ref, tri_ref, on_ref, o_ref, *, tq, tk):
    i = pl.program_id(1)
    r = tq // tk
    tri = tri_ref[...]
    tb = tri.shape[0]
    qs = [q_ref[:, h * SB_D:(h + 1) * SB_D] for h in range(SB_HEADS)]

    def step(j, carry, mask, q0=0):
        st = pl.multiple_of(j * tk, tk)
        logits = lambda h: _dot_nt(k_ref[pl.ds(st, tk), h * SB_D:(h + 1) * SB_D], qs[h][q0:])
        mask = None if mask is None else mask[:, q0:]
        out = ()
        z_next = logits(0)
        for h in range(SB_HEADS):
            run_old, acc_old = carry[2 * h:2 * h + 2]
            run, acc = run_old[:, q0:], acc_old[:, q0:]
            sl = slice(h * SB_D, (h + 1) * SB_D)
            z, z_next = z_next, (logits(h + 1) if h + 1 < SB_HEADS else None)
            log_sig = jnp.minimum(z, 0.0) - jnp.log2(1.0 + jnp.exp2(-jnp.abs(z)))
            log_not = log_sig - z
            if mask is not None:
                log_not = jnp.where(mask, log_not, 0.0)
            hi = log_not.astype(BF16)
            lo = (log_not - hi.astype(F32)).astype(BF16)
            after = []
            for blk in reversed(range(tk // tb)):
                rows = slice(blk * tb, (blk + 1) * tb)
                after.insert(0, _dot(tri, hi[rows]) + _dot(tri, lo[rows]) + run)
                run = run + jnp.sum(log_not[rows], axis=0, keepdims=True)
            a = jnp.exp2(log_sig + jnp.concatenate(after, axis=0))
            if mask is not None:
                a = jnp.where(mask, a, 0.0)
            acc = acc + _dot(vt_ref[j, sl, :], a.astype(BF16))
            out += (_lanes([run_old[:, :q0], run]), _lanes([acc_old[:, :q0], acc])) if q0 else (run, acc)
        return out

    carry = (jnp.zeros((1, tq), F32), jnp.zeros((SB_D, tq), F32)) * SB_HEADS
    for d in reversed(range(r)):
        carry = step(i * r + d, carry, _causal_mask(tk, tq, d, strict=True), q0=d * tk)
    carry = lax.fori_loop(0, i * r, lambda n, c: step(i * r - 1 - n, c, None), carry)
    for h in range(SB_HEADS):
        o_ref[:, h * SB_D:(h + 1) * SB_D] = _rms_cols(carry[2 * h + 1], on_ref[...]).T.astype(o_ref.dtype)


def _sb_attn(qk, vt, tri, on, tq, tk):
    b, s, _ = qk.shape
    w = SB_HEADS * SB_D
    return pl.pallas_call(
        functools.partial(_sb_attn_kernel, tq=tq, tk=tk),
        grid=(b, s // tq),
        in_specs=[pl.BlockSpec((None, tq, w), lambda bi, i: (bi, i, 0)),
                  pl.BlockSpec((None, s, w), lambda bi, i: (bi, 0, 1)),
                  pl.BlockSpec((None,) + vt.shape[1:], lambda bi, i: (bi, 0, 0, 0)),
                  _full(tri, 2), _full(on, 2)],
        out_specs=pl.BlockSpec((None, tq, w), lambda bi, i: (bi, i, 0)),
        out_shape=jax.ShapeDtypeStruct((b, s, w), BF16),
        compiler_params=_params("parallel", "arbitrary"),
        name="sb_attn",
    )(qk, qk, vt, tri, on)


def _topk_mask(score, n):
    blk = lax.broadcasted_iota(jnp.int32, score.shape, 0).astype(F32)

    def body(_, c):
        work, sel = c
        mx = jnp.max(work, axis=0, keepdims=True)
        idx = jnp.min(jnp.where(work == mx, blk, 1e9), axis=0, keepdims=True)
        pick = blk == idx
        return jnp.where(pick, -jnp.inf, work), jnp.where(pick, 1.0, sel)

    _, sel = lax.fori_loop(0, n, body, (score, jnp.zeros_like(score)), unroll=True)
    return sel


def _nsa_attn_kernel(q_ref, gt_ref, kc_ref, vct_ref, ks_ref, kw_ref, vst_ref, vwt_ref, ovt_ref, ext_ref,
                     on_ref, o_ref, selb_ref, *, t, tk, n_sb, n_sel):
    i = pl.program_id(1)
    t0 = i * t
    ncp = kc_ref.shape[0]
    nh = NSA_HEADS
    q4 = jnp.concatenate([q_ref[:, h * LANES:(h + 1) * LANES] for h in range(nh)], axis=0)
    qpos1 = t0 + lax.broadcasted_iota(jnp.int32, (1, t), 1)
    qpos4 = _lanes([qpos1] * nh)
    slope4 = _lanes([jnp.full((1, t), NSA_SLOPES[h], F32) for h in range(nh)])

    c_end = NSA_CMP_STRIDE * lax.broadcasted_iota(jnp.int32, (ncp, 1), 0) + (NSA_CMP_LEN - 1)
    cmask = c_end <= qpos4
    s = jnp.where(cmask, _dot_nt(kc_ref[...], q4) + slope4 * (c_end - t0).astype(F32), NEG_INF)
    p = _softmax_exact(s, cmask)
    o_cmp = _dot(vct_ref[...], p.astype(BF16))
    pc_sum = p[:, 0:t]
    for h in range(1, nh):
        pc_sum = pc_sum + p[:, h * t:(h + 1) * t]
    ovt = ovt_ref[...]
    hi = pc_sum.astype(BF16)
    r1 = pc_sum - hi.astype(F32)
    mid = r1.astype(BF16)
    lo = (r1 - mid.astype(F32)).astype(BF16)
    imp = _dot(ovt, hi) + _dot(ovt, mid) + _dot(ovt, lo)

    jb = lax.broadcasted_iota(jnp.int32, (ovt.shape[0], 1), 0)
    cur = qpos1 >> int(math.log2(NSA_SEL_BLOCK))
    valid = (jb * NSA_SEL_BLOCK <= qpos1) & (jb < n_sb)
    forced = (jb == 0) | (jb == cur) | (jb == cur - 1)
    score = jnp.where(valid, imp + jnp.where(forced, NSA_FORCE_BONUS, 0.0), NEG_INF)
    unselected = (jnp.where(score > 0.5 * NEG_INF, _topk_mask(score, n_sel), 0.0) - 1.0).astype(BF16)

    nwin = NSA_WINDOW // t + 1
    wblk = jnp.maximum(i - NSA_WINDOW // t, 0)
    w0 = pl.multiple_of(wblk * t, t)
    wcol = w0 + lax.broadcasted_iota(jnp.int32, (nwin * t, 1), 0)
    dist = qpos4 - wcol
    wmask = (dist >= 0) & (dist < NSA_WINDOW)
    s = _dot_nt(kw_ref[pl.ds(w0, nwin * t), :], q4) + slope4 * (wcol - t0).astype(F32)
    s = jnp.where(wmask, s, NEG_INF)
    e = jnp.exp2(s - jnp.max(s, axis=0, keepdims=True))
    p = (e * (1.0 / jnp.maximum(jnp.sum(e, axis=0, keepdims=True), 1e-30))).astype(BF16)
    o_win = _dot(vwt_ref[wblk], p[0:t])
    for c in range(1, nwin):
        o_win = o_win + _dot(vwt_ref[wblk + c], p[c * t:(c + 1) * t])

    for jj in range(selb_ref.shape[0]):
        selb_ref[jj] = _dot(ext_ref[jj * tk:(jj + 1) * tk, :], unselected)

    def sel_step(j, c, causal):
        st = pl.multiple_of(j * tk, tk)
        col = st + lax.broadcasted_iota(jnp.int32, (tk, 1), 0)
        s = (_dot_nt(ks_ref[pl.ds(st, tk), :], q4) + slope4 * (col - t0).astype(F32)
             + _lanes([selb_ref[j]] * nh))
        if causal:
            s = jnp.where(col <= qpos4, s, NEG_INF)
        return _softmax_step(s, vst_ref[j], *c)

    n_full = t0 // tk
    c = lax.fori_loop(0, n_full, lambda j, c: sel_step(j, c, False), _softmax_init(nh * t, NSA_D))
    _, l, acc = sel_step(n_full, c, True)
    o_sel = _softmax_out(l, acc)

    gates = gt_ref[...]
    for h in range(nh):
        cols = slice(h * t, (h + 1) * t)
        o = (gates[3 * h:3 * h + 1] * o_cmp[:, cols] + gates[3 * h + 1:3 * h + 2] * o_sel[:, cols]
             + gates[3 * h + 2:3 * h + 3] * o_win[:, cols])
        o_ref[:, h * LANES:(h + 1) * LANES] = _rms_cols(o, on_ref[...]).T.astype(o_ref.dtype)


def _nsa_attn(q, gt, kc, vct, kk, vst, vwt, ovt, ext, on, t, tk):
    b, s, _ = q.shape
    ncp = kc.shape[1]
    n_sb = s // NSA_SEL_BLOCK
    n_sel = min(NSA_N_SELECT, n_sb)
    k_spec = lambda n: pl.BlockSpec((None, s, NSA_D), lambda bi, i: (bi, 0, n))
    whole = lambda a: pl.BlockSpec((None,) + a.shape[1:], lambda bi, i: (bi,) + (0,) * (a.ndim - 1))
    return pl.pallas_call(
        functools.partial(_nsa_attn_kernel, t=t, tk=tk, n_sb=n_sb, n_sel=n_sel),
        grid=(b, s // t),
        in_specs=[pl.BlockSpec((None, t, NSA_HEADS * NSA_D), lambda bi, i: (bi, i, 0)),
                  pl.BlockSpec((None, None, LANES, t), lambda bi, i: (bi, i, 0, 0)),
                  whole(kc), whole(vct), k_spec(0), k_spec(1), whole(vst), whole(vwt),
                  _full(ovt, 2), _full(ext, 2), _full(on, 2)],
        out_specs=pl.BlockSpec((None, t, NSA_HEADS * NSA_D), lambda bi, i: (bi, i, 0)),
        out_shape=jax.ShapeDtypeStruct((b, s, NSA_HEADS * NSA_D), BF16),
        scratch_shapes=[pltpu.VMEM((s // tk, tk, t), F32)],
        compiler_params=_params("parallel", "arbitrary"),
        name="nsa_attn",
    )(q, gt, kc, vct, kk, kk, vst, vwt, ovt, ext, on)


def _pad_cols(w, width):
    return jnp.pad(w, ((0, 0), (0, width - w.shape[1])))


def _swap_halves(w):
    half = w.shape[-1] // 2
    return jnp.concatenate([w[..., half:], w[..., :half]], axis=-1)


def _row(v):
    return v.reshape(1, -1).astype(F32)


def _col(v):
    return v.reshape(-1, 1).astype(F32)


def _nsa_constants(s):
    ncp = s // NSA_CMP_STRIDE
    n_sb = s // NSA_SEL_BLOCK
    nb = -(-n_sb // 16) * 16
    c_start = NSA_CMP_STRIDE * np.arange(ncp)[None, :]
    sel_start = NSA_SEL_BLOCK * np.arange(nb)[:, None]
    overlap_t = ((c_start < sel_start + NSA_SEL_BLOCK) & (c_start + NSA_CMP_LEN > sel_start)
                 & (np.arange(nb)[:, None] < n_sb))
    expand_t = (np.arange(s)[:, None] // NSA_SEL_BLOCK) == np.arange(nb)[None, :]
    return jnp.asarray(overlap_t, BF16), jnp.asarray(expand_t * MASK_BIG, BF16)


def kernel(x, ffn1_norm, ffn1_w_gate, ffn1_w_up, ffn1_w_down, mix_norm, w_in, mla_cq_norm, mla_ckv_norm, mla_w_uq, mla_w_ukv, mla_qn_norm, mla_qr_norm, mla_kn_norm, mla_kr_norm, mla_o_norm, diff_q_norm, diff_k_norm, diff_lq1, diff_lk1, diff_lq2, diff_lk2, diff_subln, nsa_q_norm, nsa_pe_k, nsa_w_ck, nsa_pe_v, nsa_w_cv, nsa_kc_norm, nsa_ks_norm, nsa_kw_norm, nsa_o_norm, sb_o_norm, w_out, ffn2_norm, ffn2_w_gate, ffn2_w_up, ffn2_w_down):
    b, s, d = x.shape
    m = b * s
    depth = w_in.shape[0]

    pos = np.arange(s, dtype=np.float64)
    inv_freq = ROPE_THETA ** (-np.arange(0, MLA_ROPE, 2, dtype=np.float64) / MLA_ROPE)
    ang = pos[:, None] * inv_freq[None, :]
    cos, sin = np.cos(ang), np.sin(ang)
    zeros = np.zeros((s, LANES - MLA_ROPE))
    cos2 = jnp.asarray(np.concatenate([cos, cos, zeros], axis=-1), F32)
    sin2 = jnp.asarray(np.concatenate([-sin, sin, zeros], axis=-1), F32)
    tri = jnp.asarray(np.arange(SB_TRI)[None, :] > np.arange(SB_TRI)[:, None], BF16)
    overlap_t, expand_t = _nsa_constants(s)

    h = x.reshape(m, d)
    for l in range(depth):
        lambda_init = 0.8 - 0.6 * math.exp(-0.3 * l)
        h = _ffn(h, ffn1_norm[l], ffn1_w_gate[l].astype(BF16), ffn1_w_up[l].astype(BF16),
                 ffn1_w_down[l].astype(BF16))

        g_mix = _row(mix_norm[l])
        wl = w_in[l].astype(BF16)
        o0 = 0
        w_mla, o0 = wl[:, o0:o0 + W_MLA], o0 + W_MLA
        w_diff, o0 = wl[:, o0:o0 + W_DIFF], o0 + W_DIFF
        w_nsa, o0 = wl[:, o0:o0 + W_NSA], o0 + W_NSA
        w_sb = wl[:, o0:o0 + W_SB]
        per_batch = lambda a: a.reshape((b, a.shape[0] // b) + a.shape[1:])

        w_kr = w_mla[:, MLA_Q_RANK + MLA_KV_RANK:]
        w_mla_p = jnp.concatenate([w_mla[:, :MLA_Q_RANK + MLA_KV_RANK], _pad_cols(w_kr, LANES),
                                   _pad_cols(_swap_halves(w_kr), LANES)], axis=1)
        wuq = mla_w_uq[l].reshape(MLA_Q_RANK, MLA_HEADS, MLA_NOPE + MLA_ROPE)
        wuq_r = wuq[..., MLA_NOPE:]
        pad3 = lambda a: jnp.pad(a, ((0, 0), (0, 0), (0, LANES - a.shape[-1])))
        wuq_p = jnp.concatenate([wuq[..., :MLA_NOPE], pad3(wuq_r), pad3(_swap_halves(wuq_r))],
                                axis=-1).reshape(MLA_Q_RANK, MLA_HEADS * 3 * LANES).astype(BF16)
        padr = lambda v: _pad_cols(_row(v), LANES)
        q_mla, k_mla, vt_mla = _mla_prep(
            h.reshape(b, s, d), g_mix, w_mla_p, _row(mla_cq_norm[l]), _row(mla_ckv_norm[l]), wuq_p,
            mla_w_ukv[l].astype(BF16), _row(mla_qn_norm[l]), padr(mla_qr_norm[l]),
            padr(_swap_halves(mla_qr_norm[l])), _row(mla_kn_norm[l]), padr(mla_kr_norm[l]),
            padr(_swap_halves(mla_kr_norm[l])), cos2, sin2, tm=MLA_TK)
        o_a = _mla_attn(q_mla, k_mla, vt_mla, _col(mla_o_norm[l]), MLA_TQ, MLA_TK)

        two = lambda v: jnp.concatenate([_row(v), _row(v)], axis=1)
        q1, q2, k_d, vt_d = _diff_prep(h, g_mix, w_diff, two(diff_q_norm[l]),
                                       two(diff_k_norm[l]), tm=DIFF_TK)
        o_b = _diff_attn(per_batch(q1), per_batch(q2), per_batch(k_d), per_batch(vt_d), _row(diff_lq1[l]),
                         _row(diff_lk1[l]), _row(diff_lq2[l]), _row(diff_lk2[l]), _col(diff_subln[l]),
                         lambda_init, DIFF_TQ, DIFF_TK)

        w_nsa_p = _pad_cols(w_nsa, NSA_HEADS * LANES + 7 * LANES)
        q_n, k_n, raw_n, vst_n, vwt_n, gt_n = _nsa_prep(
            h, g_mix, w_nsa_p, _row(nsa_q_norm[l]), _row(nsa_ks_norm[l]), _row(nsa_kw_norm[l]), tm=NSA_TK, tq=NSA_TQ)
        kc, vct = _nsa_compress(per_batch(raw_n), nsa_pe_k[l], nsa_pe_v[l], nsa_w_ck[l].astype(BF16),
                                nsa_w_cv[l].astype(BF16), _row(nsa_kc_norm[l]))
        o_c = _nsa_attn(per_batch(q_n), per_batch(gt_n), kc, vct, per_batch(k_n), per_batch(vst_n),
                        per_batch(vwt_n), overlap_t, expand_t, _col(nsa_o_norm[l]), NSA_TQ, NSA_TK)

        qk_sb, vt_sb = _sb_prep(h, g_mix, w_sb, tm=SB_TK)
        o_d = _sb_attn(per_batch(qk_sb), per_batch(vt_sb), tri, _col(sb_o_norm[l]), SB_TQ, SB_TK)

        flat = lambda a: a.reshape(m, -1)
        h = _out_proj([flat(o_a), flat(o_b), flat(o_c), flat(o_d)], w_out[l].astype(BF16), h)
        h = _ffn(h, ffn2_norm[l], ffn2_w_gate[l].astype(BF16), ffn2_w_up[l].astype(BF16),
                 ffn2_w_down[l].astype(BF16))
    return h.reshape(b, s, d)
```
